```python
import jax
import jax.numpy as jnp
from jax import lax
import numpy as np

D_MODEL = 1024
BATCH = 8
SEQ = 2048
DEPTH = 4
DEC_BATCH = 128
DEC_SEQ = 1
PAST_LEN = 2048
PAGE_SIZE = 128

N_A_LAYERS = DEPTH // 2
N_B_LAYERS = DEPTH - N_A_LAYERS
D_RNN = D_MODEL
LRU_HEADS = 8
LRU_BLOCK = D_RNN // LRU_HEADS
LRU_CONV_W = 4
LRU_C = 8.0
N_HEADS = 16
HEAD_DIM = 64
N_KV_HEADS = 4
GROUP = N_HEADS // N_KV_HEADS
Q_WIDTH = N_HEADS * HEAD_DIM
L_CMP = 32
D_CMP = 16
CMP_HIDDEN = 128
L_SEL = 64
TOP_N = 8
WINDOW = 512
Q_BLOCK = 128
ROT_DIM = HEAD_DIM // 4
ROPE_THETA = 500000.0
N_KV_SLOTS = 4
D_FF = 3 * D_MODEL
FFN_CONV_W = 3
N_MOD = 6
EPS = 1e-6
NEG_INF = -1e30
FORCED_SCORE = 1e6

kernel_name = 'hawk_nsa_yoco_decode_step'


def rmsnorm(x, g):
    xf = x.astype(jnp.float32)
    y = xf * lax.rsqrt(jnp.mean(xf * xf, axis=-1, keepdims=True) + EPS)
    return (y * g.astype(jnp.float32)).astype(x.dtype)


def ada_mod(c, w, b, n):
    m = jax.nn.silu(c) @ w + b
    return m.reshape(c.shape[0], n, 1, D_MODEL)


def causal_dwconv(u, buf, w, b):
    t = u.shape[1]
    xp = jnp.concatenate([buf.astype(u.dtype), u], axis=1)
    out = b + sum(w[k] * xp[:, k:k + t] for k in range(w.shape[0]))
    return out, xp[:, t:]


def rope_partial(x, pos):
    half = ROT_DIM // 2
    inv = jnp.power(ROPE_THETA, -jnp.arange(half, dtype=jnp.float32) * (2.0 / ROT_DIM))
    ang = pos.astype(jnp.float32)[:, None] * inv[None, :]
    cos = jnp.cos(ang)[:, None, :].astype(x.dtype)
    sin = jnp.sin(ang)[:, None, :].astype(x.dtype)
    x1 = x[..., :half]
    x2 = x[..., half:ROT_DIM]
    return jnp.concatenate([x1 * cos - x2 * sin, x2 * cos + x1 * sin, x[..., ROT_DIM:]], axis=-1)


def masked_softmax(s, mask):
    s = jnp.where(mask, s, NEG_INF)
    p = jax.nn.softmax(s, axis=-1)
    return jnp.where(mask, p, 0.0)


def lru_combine(e1, e2):
    a1, b1 = e1
    a2, b2 = e2
    return a1 * a2, a2 * b1 + b2


def rglru_block(h, h0, conv_buf, w_in, b_in, w_conv, b_conv, w_gate, b_gate, lam, w_out):
    bsz, t, _ = h.shape
    proj = h @ w_in + b_in
    y_branch = jax.nn.gelu(proj[..., :D_RNN])
    u, new_buf = causal_dwconv(proj[..., D_RNN:], conv_buf, w_conv, b_conv)
    ub = u.reshape(bsz, t, LRU_HEADS, LRU_BLOCK)
    gates = jnp.einsum('bthi,ghij->gbthj', ub, w_gate).reshape(2, bsz, t, D_RNN) + b_gate[:, None, None, :]
    r = jax.nn.sigmoid(gates[0].astype(jnp.float32))
    i = jax.nn.sigmoid(gates[1].astype(jnp.float32))
    log_a = -LRU_C * r * jax.nn.softplus(-lam.astype(jnp.float32))
    a = jnp.exp(log_a)
    bx = jnp.sqrt(-jnp.expm1(2.0 * log_a)) * (i * u.astype(jnp.float32))
    bx = bx.at[:, 0].add(a[:, 0] * h0.astype(jnp.float32))
    _, hs = lax.associative_scan(lru_combine, (a, bx), axis=1)
    out = (hs.astype(h.dtype) * y_branch) @ w_out
    return out, hs[:, -1].astype(h.dtype), new_buf


def conv_ffn(h, buf, w_in, w_conv, b_conv, w_out):
    z, new_buf = causal_dwconv(h @ w_in, buf, w_conv, b_conv)
    return (jax.nn.gelu(z[..., :D_FF]) * z[..., D_FF:]) @ w_out, new_buf


def shared_kv(x, c, pos, g_kv, w_mod_kv, b_mod_kv, w_kv):
    bsz, t, _ = x.shape
    m = ada_mod(c, w_mod_kv, b_mod_kv, 2)
    h = rmsnorm(x, g_kv) * (1.0 + m[:, 1]) + m[:, 0]
    kv = (h @ w_kv).reshape(bsz, t, 6, N_KV_HEADS, HEAD_DIM)
    rows = jnp.stack([kv[:, :, 0], kv[:, :, 1], rope_partial(kv[:, :, 2], pos), kv[:, :, 3]], axis=2)
    win = jnp.stack([rope_partial(kv[:, :, 4], pos), kv[:, :, 5]], axis=2)
    return rows, win


def compress(seq_rows, pe, w1, b1, w2):
    bsz, t = seq_rows.shape[:2]
    nc = (t - L_CMP) // D_CMP + 1
    idx = np.arange(nc)[:, None] * D_CMP + np.arange(L_CMP)[None, :]
    blk = seq_rows[:, idx] + pe[None, None, :, None, :]
    blk = blk.transpose(0, 1, 3, 2, 4).reshape(bsz, nc, N_KV_HEADS, L_CMP * HEAD_DIM)
    return jax.nn.gelu(blk @ w1 + b1) @ w2


def sel_blocks(seq_rows, ns):
    bsz, t = seq_rows.shape[:2]
    xp = jnp.pad(seq_rows, ((0, 0), (0, ns * L_SEL - t), (0, 0), (0, 0)))
    return xp.reshape(bsz, ns, L_SEL, N_KV_HEADS, HEAD_DIM).transpose(0, 3, 1, 2, 4)


def nsa_context(rows, pe_cmp, w_cmp1, b_cmp1, w_cmp2):
    t = rows.shape[1]
    nc = (t - L_CMP) // D_CMP + 1
    ns = -(-t // L_SEL)
    c_start = np.arange(nc) * D_CMP
    s_start = np.arange(ns) * L_SEL
    cmp_to_sel = ((c_start[:, None] < s_start[None, :] + L_SEL)
                  & (c_start[:, None] + L_CMP > s_start[None, :])).astype(np.float32)
    return {
        'kc': compress(rows[:, :, 0], pe_cmp[0], w_cmp1[0], b_cmp1[0], w_cmp2[0]),
        'vc': compress(rows[:, :, 1], pe_cmp[1], w_cmp1[1], b_cmp1[1], w_cmp2[1]),
        'cend': jnp.asarray(c_start + L_CMP - 1, dtype=jnp.int32),
        'ks': sel_blocks(rows[:, :, 2], ns),
        'vs': sel_blocks(rows[:, :, 3], ns),
        'cmp_to_sel': jnp.asarray(cmp_to_sel),
    }


def nsa_attend(q_rot, q_raw, gates, q_pos, ctx, kw, vw, kw_pos):
    bsz, nq = q_rot.shape[:2]
    scale = HEAD_DIM ** -0.5
    qr = q_raw.reshape(bsz, nq, N_KV_HEADS, GROUP, HEAD_DIM)
    qs = q_rot.reshape(bsz, nq, N_KV_HEADS, GROUP, HEAD_DIM)
    s_c = jnp.einsum('bcgrd,bngd->bcgrn', qr, ctx['kc']).astype(jnp.float32) * scale
    m_c = (ctx['cend'][None, :] <= q_pos[:, None])[None, :, None, None, :]
    p_c = masked_softmax(s_c, m_c)
    o_c = jnp.einsum('bcgrn,bngd->bcgrd', p_c, ctx['vc'].astype(jnp.float32))
    ns = ctx['ks'].shape[2]
    imp = jnp.einsum('bcgrn,nj->bcgj', p_c, ctx['cmp_to_sel'])
    blk = jnp.arange(ns)[None, :]
    cur = (q_pos // L_SEL)[:, None]
    forced = (blk == 0) | (blk == cur) | (blk == cur - 1)
    imp = jnp.where(forced[None, :, None, :], FORCED_SCORE, imp)
    imp = jnp.where((blk <= cur)[None, :, None, :], imp, NEG_INF)
    n_top = min(TOP_N, ns)
    top_val, top_idx = lax.top_k(imp, n_top)
    b_ix = jnp.arange(bsz)[:, None, None, None]
    g_ix = jnp.arange(N_KV_HEADS)[None, None, :, None]
    n_sel = n_top * L_SEL
    k_g = ctx['ks'][b_ix, g_ix, top_idx].reshape(bsz, nq, N_KV_HEADS, n_sel, HEAD_DIM)
    v_g = ctx['vs'][b_ix, g_ix, top_idx].reshape(bsz, nq, N_KV_HEADS, n_sel, HEAD_DIM)
    k_pos = top_idx[..., None] * L_SEL + jnp.arange(L_SEL)
    m_s = ((top_val > 0.5 * NEG_INF)[..., None]
           & (k_pos <= q_pos[None, :, None, None, None])).reshape(bsz, nq, N_KV_HEADS, 1, n_sel)
    s_s = jnp.einsum('bcgrd,bcgkd->bcgrk', qs, k_g).astype(jnp.float32) * scale
    p_s = masked_softmax(s_s, m_s)
    o_s = jnp.einsum('bcgrk,bcgkd->bcgrd', p_s, v_g.astype(jnp.float32))
    dpos = q_pos[:, None] - kw_pos[None, :]
    m_w = ((kw_pos[None, :] >= 0) & (dpos >= 0) & (dpos < WINDOW))[None, :, None, None, :]
    s_w = jnp.einsum('bcgrd,bkgd->bcgrk', qs, kw).astype(jnp.float32) * scale
    p_w = masked_softmax(s_w, m_w)
    o_w = jnp.einsum('bcgrk,bkgd->bcgrd', p_w, vw.astype(jnp.float32))
    g = gates.reshape(bsz, nq, N_KV_HEADS, GROUP, 3)
    o = g[..., 0:1] * o_c + g[..., 1:2] * o_s + g[..., 2:3] * o_w
    return o.reshape(bsz, nq, Q_WIDTH)


def nsa_mix(h, q_pos, ctx, w_qg, w_o, banded):
    bsz, t, _ = h.shape
    proj = h @ w_qg
    q = proj[..., :Q_WIDTH].reshape(bsz, t, N_HEADS, HEAD_DIM)
    gates = jax.nn.sigmoid(proj[..., Q_WIDTH:].astype(jnp.float32)).reshape(bsz, t, N_HEADS, 3)
    q_rot = rope_partial(q, q_pos)
    if banded:
        nblk = t // Q_BLOCK
        kw_pad = jnp.pad(ctx['kw'], ((0, 0), (WINDOW, 0), (0, 0), (0, 0)))
        vw_pad = jnp.pad(ctx['vw'], ((0, 0), (WINDOW, 0), (0, 0), (0, 0)))

        def to_blocks(a):
            return a.reshape((bsz, nblk, Q_BLOCK) + a.shape[2:]).swapaxes(0, 1)

        def one_block(args):
            i, qr_b, qx_b, g_b, p_b = args
            start = i * Q_BLOCK
            kw = lax.dynamic_slice_in_dim(kw_pad, start, WINDOW + Q_BLOCK, axis=1)
            vw = lax.dynamic_slice_in_dim(vw_pad, start, WINDOW + Q_BLOCK, axis=1)
            kw_pos = start - WINDOW + jnp.arange(WINDOW + Q_BLOCK)
            return nsa_attend(qr_b, qx_b, g_b, p_b, ctx, kw, vw, kw_pos)

        o = lax.map(one_block, (jnp.arange(nblk), to_blocks(q_rot), to_blocks(q), to_blocks(gates),
                                q_pos.reshape(nblk, Q_BLOCK)))
        o = o.swapaxes(0, 1).reshape(bsz, t, Q_WIDTH)
    else:
        o = nsa_attend(q_rot, q, gates, q_pos, ctx, ctx['kw'], ctx['vw'], ctx['kw_pos'])
    return o.astype(h.dtype) @ w_o


def trunk(x, c, pos, lru_h0, lru_conv0, ffn_conv0, build_ctx, banded, lw):
    lru_h, lru_conv, ffn_conv = [], [], []
    ctx, kv_state = None, None
    for l in range(DEPTH):
        m = ada_mod(c, lw['w_mod'][l], lw['b_mod'][l], N_MOD)
        g = lw['g_norm'][l]
        h = rmsnorm(x, g[0]) * (1.0 + m[:, 1]) + m[:, 0]
        if l < N_A_LAYERS:
            o, h_last, cbuf = rglru_block(h, lru_h0[l], lru_conv0[l], lw['w_lru_in'][l], lw['b_lru_in'][l],
                                          lw['w_lru_conv'][l], lw['b_lru_conv'][l], lw['w_lru_gate'][l],
                                          lw['b_lru_gate'][l], lw['lru_lambda'][l], lw['w_lru_out'][l])
            lru_h.append(h_last)
            lru_conv.append(cbuf)
        else:
            j = l - N_A_LAYERS
            o = nsa_mix(h, pos, ctx, lw['w_nsa_qg'][j], lw['w_nsa_out'][j], banded)
        x = x + m[:, 2] * rmsnorm(o, g[1])
        h = rmsnorm(x, g[2]) * (1.0 + m[:, 4]) + m[:, 3]
        o, fbuf = conv_ffn(h, ffn_conv0[l], lw['w_ffn_in'][l], lw['w_ffn_conv'][l], lw['b_ffn_conv'][l],
                           lw['w_ffn_out'][l])
        ffn_conv.append(fbuf)
        x = x + m[:, 5] * rmsnorm(o, g[3])
        if l == N_A_LAYERS - 1:
            ctx, kv_state = build_ctx(x, c)
    return x, kv_state, jnp.stack(lru_h), jnp.stack(lru_conv), jnp.stack(ffn_conv)


def setup_inputs(seed: int = 0) -> dict:
    key = jax.random.key(seed)
    ks = iter(jax.random.split(key, 48))

    def nrm(shape, s):
        return jax.random.normal(next(ks), shape, jnp.float32) * s

    n_pages = PAST_LEN // PAGE_SIZE
    n_phys = (DEC_BATCH * n_pages * 5) // 4
    perm = jax.random.permutation(next(ks), n_phys)
    page_table = perm[:DEC_BATCH * n_pages].reshape(DEC_BATCH, n_pages).astype(jnp.int32)
    win_buf = min(WINDOW, PAST_LEN)
    a0 = jax.random.uniform(next(ks), (N_A_LAYERS, D_RNN), jnp.float32, 0.9, 0.999)
    lru_lambda = jnp.log(a0) - jnp.log1p(-a0)
    d_inv = D_MODEL ** -0.5
    return {
        'x_prompt': nrm((BATCH, SEQ, D_MODEL), 1.0),
        'x_sample': nrm((DEC_BATCH, DEC_SEQ, D_MODEL), 1.0),
        'c_prompt': nrm((BATCH, D_MODEL), 1.0),
        'c_sample': nrm((DEC_BATCH, D_MODEL), 1.0),
        'cache_nsa_kv': nrm((n_phys, PAGE_SIZE, N_KV_SLOTS, N_KV_HEADS, HEAD_DIM), 1.0),
        'page_table': page_table,
        'state_nsa_win': nrm((DEC_BATCH, win_buf, 2, N_KV_HEADS, HEAD_DIM), 1.0),
        'state_lru_h': nrm((N_A_LAYERS, DEC_BATCH, D_RNN), 0.5),
        'state_lru_conv': nrm((N_A_LAYERS, DEC_BATCH, LRU_CONV_W - 1, D_RNN), 1.0),
        'state_ffn_conv': nrm((DEPTH, DEC_BATCH, FFN_CONV_W - 1, 2 * D_FF), 1.0),
        'w_mod': nrm((DEPTH, D_MODEL, N_MOD * D_MODEL), 0.5 * d_inv),
        'b_mod': nrm((DEPTH, N_MOD * D_MODEL), 0.01),
        'g_norm': 1.0 + nrm((DEPTH, 4, D_MODEL), 0.02),
        'w_lru_in': nrm((N_A_LAYERS, D_MODEL, 2 * D_RNN), d_inv),
        'b_lru_in': nrm((N_A_LAYERS, 2 * D_RNN), 0.01),
        'w_lru_conv': nrm((N_A_LAYERS, LRU_CONV_W, D_RNN), LRU_CONV_W ** -0.5),
        'b_lru_conv': nrm((N_A_LAYERS, D_RNN), 0.01),
        'w_lru_gate': nrm((N_A_LAYERS, 2, LRU_HEADS, LRU_BLOCK, LRU_BLOCK), LRU_BLOCK ** -0.5),
        'b_lru_gate': nrm((N_A_LAYERS, 2, D_RNN), 0.01),
        'lru_lambda': lru_lambda,
        'w_lru_out': nrm((N_A_LAYERS, D_RNN, D_MODEL), D_RNN ** -0.5),
        'g_kv': 1.0 + nrm((D_MODEL,), 0.02),
        'w_mod_kv': nrm((D_MODEL, 2 * D_MODEL), 0.5 * d_inv),
        'b_mod_kv': nrm((2 * D_MODEL,), 0.01),
        'w_kv': nrm((D_MODEL, 6 * N_KV_HEADS * HEAD_DIM), d_inv),
        'pe_cmp': nrm((2, L_CMP, HEAD_DIM), 0.1),
        'w_cmp1': nrm((2, L_CMP * HEAD_DIM, CMP_HIDDEN), (L_CMP * HEAD_DIM) ** -0.5),
        'b_cmp1': nrm((2, CMP_HIDDEN), 0.01),
        'w_cmp2': nrm((2, CMP_HIDDEN, HEAD_DIM), CMP_HIDDEN ** -0.5),
        'w_nsa_qg': nrm((N_B_LAYERS, D_MODEL, Q_WIDTH + 3 * N_HEADS), d_inv),
        'w_nsa_out': nrm((N_B_LAYERS, Q_WIDTH, D_MODEL), Q_WIDTH ** -0.5),
        'w_ffn_in': nrm((DEPTH, D_MODEL, 2 * D_FF), d_inv),
        'w_ffn_conv': nrm((DEPTH, FFN_CONV_W, 2 * D_FF), FFN_CONV_W ** -0.5),
        'b_ffn_conv': nrm((DEPTH, 2 * D_FF), 0.01),
        'w_ffn_out': nrm((DEPTH, D_FF, D_MODEL), D_FF ** -0.5),
    }


def reference(x_prompt, x_sample, c_prompt, c_sample, cache_nsa_kv, page_table, state_nsa_win,
              state_lru_h, state_lru_conv, state_ffn_conv, w_mod, b_mod, g_norm,
              w_lru_in, b_lru_in, w_lru_conv, b_lru_conv, w_lru_gate, b_lru_gate, lru_lambda, w_lru_out,
              g_kv, w_mod_kv, b_mod_kv, w_kv, pe_cmp, w_cmp1, b_cmp1, w_cmp2,
              w_nsa_qg, w_nsa_out, w_ffn_in, w_ffn_conv, b_ffn_conv, w_ffn_out):
    lw = {'w_mod': w_mod, 'b_mod': b_mod, 'g_norm': g_norm, 'w_lru_in': w_lru_in, 'b_lru_in': b_lru_in,
          'w_lru_conv': w_lru_conv, 'b_lru_conv': b_lru_conv, 'w_lru_gate': w_lru_gate,
          'b_lru_gate': b_lru_gate, 'lru_lambda': lru_lambda, 'w_lru_out': w_lru_out,
          'w_nsa_qg': w_nsa_qg, 'w_nsa_out': w_nsa_out, 'w_ffn_in': w_ffn_in,
          'w_ffn_conv': w_ffn_conv, 'b_ffn_conv': b_ffn_conv, 'w_ffn_out': w_ffn_out}

    bsz_p, t_p, _ = x_prompt.shape
    pos_p = jnp.arange(t_p, dtype=jnp.int32)

    def prompt_ctx(x, c):
        rows, win = shared_kv(x, c, pos_p, g_kv, w_mod_kv, b_mod_kv, w_kv)
        ctx = nsa_context(rows, pe_cmp, w_cmp1, b_cmp1, w_cmp2)
        ctx.update(kw=win[:, :, 0], vw=win[:, :, 1], kw_pos=pos_p)
        return ctx, (rows, win[:, t_p - min(WINDOW, t_p):])

    dt = x_prompt.dtype
    y_prompt, (rows_p, win_p), lh_p, lc_p, fc_p = trunk(
        x_prompt, c_prompt, pos_p,
        jnp.zeros((N_A_LAYERS, bsz_p, D_RNN), dt),
        jnp.zeros((N_A_LAYERS, bsz_p, LRU_CONV_W - 1, D_RNN), dt),
        jnp.zeros((DEPTH, bsz_p, FFN_CONV_W - 1, 2 * D_FF), dt),
        prompt_ctx, True, lw)

    bsz_s, t_s, _ = x_sample.shape
    pos_s = PAST_LEN + jnp.arange(t_s, dtype=jnp.int32)

    def sample_ctx(x, c):
        rows, win = shared_kv(x, c, pos_s, g_kv, w_mod_kv, b_mod_kv, w_kv)
        n_pages = page_table.shape[1]
        past = cache_nsa_kv[page_table].reshape((bsz_s, n_pages * PAGE_SIZE) + cache_nsa_kv.shape[2:])
        full = jnp.concatenate([past.astype(rows.dtype), rows], axis=1)
        ctx = nsa_context(full, pe_cmp, w_cmp1, b_cmp1, w_cmp2)
        win_buf = state_nsa_win.shape[1]
        win_all = jnp.concatenate([state_nsa_win.astype(win.dtype), win], axis=1)
        kw_pos = PAST_LEN - win_buf + jnp.arange(win_buf + t_s, dtype=jnp.int32)
        ctx.update(kw=win_all[:, :, 0], vw=win_all[:, :, 1], kw_pos=kw_pos)
        keep = min(WINDOW, PAST_LEN + t_s)
        return ctx, (rows, win_all[:, win_buf + t_s - keep:])

    y_sample, (rows_s, win_s), lh_s, lc_s, fc_s = trunk(
        x_sample, c_sample, pos_s, state_lru_h, state_lru_conv, state_ffn_conv,
        sample_ctx, False, lw)

    return (y_prompt, y_sample, rows_p, rows_s, win_p, win_s, lh_p, lh_s, lc_p, lc_s, fc_p, fc_s)
```

```python
import functools

import numpy as np
import jax
import jax.numpy as jnp
from jax import lax
from jax.experimental import pallas as pl
from jax.experimental.pallas import tpu as pltpu

f32 = jnp.float32
bf16 = jnp.bfloat16

D_MODEL = 1024
D_RNN = D_MODEL
LRU_HEADS = 8
LRU_BLOCK = D_RNN // LRU_HEADS
LRU_CONV_W = 4
LRU_C = 8.0
N_HEADS = 16
HEAD_DIM = 64
N_KV_HEADS = 4
GROUP = N_HEADS // N_KV_HEADS
Q_WIDTH = N_HEADS * HEAD_DIM
KV_WIDTH = N_KV_HEADS * HEAD_DIM
L_CMP = 32
D_CMP = 16
CMP_HIDDEN = 128
L_SEL = 64
TOP_N = 8
WINDOW = 512
ROT_DIM = HEAD_DIM // 4
ROPE_THETA = 500000.0
D_FF = 3 * D_MODEL
FFN_CONV_W = 3
N_MOD = 6
EPS = 1e-6
NEG_INF = -1e30
FORCED_SCORE = 1e6
PAGE_SIZE = 128

LANES = 128
SUBLANES = 8
VMEM_LIMIT = 60 * 1024 * 1024
ROW_TILE = 256
Q_TILE = 128
K_CHUNK = 512
FF_CHUNK = 512
BELOW_ALL = -3.0e38


def _cparams(sem=None):
    return pltpu.CompilerParams(dimension_semantics=sem, vmem_limit_bytes=VMEM_LIMIT)


def _const_spec(shape):
    nd = len(shape)
    return pl.BlockSpec(shape, lambda *_: (0,) * nd, pipeline_mode=pl.Buffered(1))


def _rms(x, g):
    return x * lax.rsqrt(jnp.mean(x * x, axis=-1, keepdims=True) + EPS) * g


def _gelu(x):
    return 0.5 * x * (1.0 + jnp.tanh(0.7978845608028654 * (x + 0.044715 * (x * x * x))))


def _mm(a, w):
    return jnp.dot(a.astype(bf16), w, preferred_element_type=f32)


def _mm_nt(a, b):
    return lax.dot_general(a, b, (((1,), (1,)), ((), ())), preferred_element_type=f32)


def _mm_exact_rhs(a, b01):
    a1 = a.astype(bf16)
    r1 = a - a1.astype(f32)
    a2 = r1.astype(bf16)
    a3 = (r1 - a2.astype(f32)).astype(bf16)
    return (jnp.dot(a1, b01, preferred_element_type=f32) + jnp.dot(a2, b01, preferred_element_type=f32)
            + jnp.dot(a3, b01, preferred_element_type=f32))


def _mm_exact_lhs(a01, b):
    b1 = b.astype(bf16)
    r1 = b - b1.astype(f32)
    b2 = r1.astype(bf16)
    b3 = (r1 - b2.astype(f32)).astype(bf16)
    return (jnp.dot(a01, b1, preferred_element_type=f32) + jnp.dot(a01, b2, preferred_element_type=f32)
            + jnp.dot(a01, b3, preferred_element_type=f32))


def _shift_rows(cur, tail, k):
    r = pltpu.roll(cur, k, 0)
    rp = pltpu.roll(tail, k, 0)
    row = lax.broadcasted_iota(jnp.int32, tail.shape, 0)
    first = jnp.where(row < k, rp, r[0:SUBLANES])
    return jnp.concatenate([first, r[SUBLANES:]], axis=0)


def _causal_conv_tile(cur, tail, w, b):
    width = w.shape[0]
    out = b + w[width - 1:width] * cur
    for k in range(1, width):
        out = out + w[width - 1 - k:width - k] * _shift_rows(cur, tail, k)
    return out


def _lru_gates(u, wg_ref, bg, lam):
    pair = 2 * LRU_BLOCK
    n_pair = D_RNN // pair
    ub = u.astype(bf16)
    gates = []
    for gi in range(2):
        cols = [jnp.dot(ub[:, p * pair:(p + 1) * pair], wg_ref[gi, p], preferred_element_type=f32) for p in range(n_pair)]
        gates.append(jnp.concatenate(cols, axis=1) + bg[gi:gi + 1])
    r = jax.nn.sigmoid(gates[0])
    i = jax.nn.sigmoid(gates[1])
    z = -lam
    softplus = jnp.maximum(z, 0.0) + jnp.log1p(jnp.exp(-jnp.abs(z)))
    log_a = -LRU_C * r * softplus
    a = jnp.exp(log_a)
    mult = jnp.sqrt(-jnp.tanh(log_a) * (a * a + 1.0))
    return a, mult * (i * u)


def _scan_rows(a, b):
    n = a.shape[0]
    row = lax.broadcasted_iota(jnp.int32, a.shape, 0)
    s = 1
    while s < n:
        a_sh = jnp.where(row >= s, pltpu.roll(a, s, 0), 1.0)
        b_sh = jnp.where(row >= s, pltpu.roll(b, s, 0), 0.0)
        b = a * b_sh + b
        a = a * a_sh
        s *= 2
    return a, b


def _rope_cols(x, cos_t, sin_t):
    lane = lax.broadcasted_iota(jnp.int32, x.shape, 1)
    half = ROT_DIM // 2
    swapped = jnp.where(lane % HEAD_DIM < half, pltpu.roll(x, LANES - half, 1), pltpu.roll(x, half, 1))
    return x * cos_t + swapped * sin_t


def _rope_wide(x, cos_t, sin_t):
    n = x.shape[1] // LANES
    return jnp.concatenate([_rope_cols(x[:, j * LANES:(j + 1) * LANES], cos_t, sin_t) for j in range(n)], axis=1)


def _topk_mask(imp, cur, n_blocks):
    blk = lax.broadcasted_iota(jnp.int32, imp.shape, 1)
    forced = (blk == 0) | (blk == cur) | (blk == cur - 1)
    imp = jnp.where(forced, FORCED_SCORE, imp)
    imp = jnp.where(blk <= cur, imp, NEG_INF)
    imp = jnp.where(blk < n_blocks, imp, BELOW_ALL)
    sel = jnp.zeros(imp.shape, f32)
    for _ in range(min(TOP_N, n_blocks)):
        m = jnp.max(imp, axis=1, keepdims=True)
        idx = jnp.min(jnp.where(imp == m, blk, 1 << 20), axis=1, keepdims=True)
        pick = blk == idx
        sel = jnp.where(pick & (m > 0.5 * NEG_INF), 1.0, sel)
        imp = jnp.where(pick, BELOW_ALL, imp)
    return sel


def _softmax_step(s, mask, m_old, l_old, acc_old, v):
    s = jnp.where(mask, s, NEG_INF)
    m_new = jnp.maximum(m_old, jnp.max(s, axis=1, keepdims=True))
    alpha = jnp.exp(m_old - m_new)
    p = jnp.where(mask, jnp.exp(s - m_new), 0.0)
    l_new = alpha * l_old + jnp.sum(p, axis=1, keepdims=True)
    acc_new = alpha * acc_old + jnp.dot(p.astype(bf16), v, preferred_element_type=f32)
    return m_new, l_new, acc_new


def _mod_body(c_ref, w_ref, b_ref, o_ref):
    c = c_ref[...]
    o_ref[0] = _mm(c * jax.nn.sigmoid(c), w_ref[0].astype(bf16)) + b_ref[0]


def _ada_mod(c, w, b):
    n_l, d, n = w.shape
    m = c.shape[0]
    tn = 1024 if n % 1024 == 0 else n
    return pl.pallas_call(
        _mod_body,
        grid=(n_l, n // tn),
        in_specs=[pl.BlockSpec((m, d), lambda l, j: (0, 0)),
                  pl.BlockSpec((1, d, tn), lambda l, j: (l, 0, j)),
                  pl.BlockSpec((1, 1, tn), lambda l, j: (l, 0, j))],
        out_specs=pl.BlockSpec((1, m, tn), lambda l, j: (l, 0, j)),
        out_shape=jax.ShapeDtypeStruct((n_l, m, n), f32),
        compiler_params=_cparams(("arbitrary", "arbitrary")),
        name="ada_mod",
    )(c, w, b.reshape(n_l, 1, n))


def _lru_prompt_body(x_ref, mod_ref, g_ref, win_ref, bin_ref, wc_ref, bc_ref, wg_ref, bg_ref, lam_ref, wout_ref,
                     xo_ref, hlast_ref, cbuf_ref, tail_ref, h_ref):
    t = pl.program_id(1)

    @pl.when(t == 0)
    def _():
        tail_ref[...] = jnp.zeros(tail_ref.shape, f32)
        h_ref[...] = jnp.zeros(h_ref.shape, f32)

    x = x_ref[0]
    tm = x.shape[0]
    mod = mod_ref[0]
    g = g_ref[...]
    h = _rms(x, g[0:1]) * (1.0 + mod[1:2]) + mod[0:1]
    proj = _mm(h, win_ref[...]) + bin_ref[...]
    y = _gelu(proj[:, :D_RNN])
    up = proj[:, D_RNN:]
    u = _causal_conv_tile(up, tail_ref[...], wc_ref[...], bc_ref[...])
    last_rows = up[tm - SUBLANES:tm]
    tail_ref[...] = last_rows
    cbuf_ref[0] = last_rows
    a, bx = _lru_gates(u, wg_ref, bg_ref[...], lam_ref[...])
    a_cum, hs = _scan_rows(a, bx)
    hs = hs + a_cum * h_ref[0:1]
    h_fin = hs[tm - 1:tm]
    h_ref[...] = jnp.broadcast_to(h_fin, h_ref.shape)
    hlast_ref[0] = jnp.broadcast_to(h_fin, hlast_ref.shape[1:])
    out = _mm(hs * y, wout_ref[...])
    xo_ref[0] = x + mod[2:3] * _rms(out, g[1:2])


def _lru_prompt(x, mod, g, w_in, b_in, w_conv, b_conv, wg, bg, lam, w_out):
    bsz, t, d = x.shape
    tm = min(ROW_TILE, t)
    row_spec = pl.BlockSpec((1, tm, d), lambda b, i: (b, i, 0))
    small = lambda r, c: pl.BlockSpec((1, r, c), lambda b, i: (b, 0, 0))
    return pl.pallas_call(
        _lru_prompt_body,
        grid=(bsz, t // tm),
        in_specs=[row_spec, small(N_MOD, d), _const_spec(g.shape), _const_spec(w_in.shape), _const_spec(b_in.shape),
                  _const_spec(w_conv.shape), _const_spec(b_conv.shape), _const_spec(wg.shape), _const_spec(bg.shape),
                  _const_spec(lam.shape), _const_spec(w_out.shape)],
        out_specs=[row_spec, small(SUBLANES, D_RNN), small(SUBLANES, D_RNN)],
        out_shape=[jax.ShapeDtypeStruct((bsz, t, d), f32), jax.ShapeDtypeStruct((bsz, SUBLANES, D_RNN), f32),
                   jax.ShapeDtypeStruct((bsz, SUBLANES, D_RNN), f32)],
        scratch_shapes=[pltpu.VMEM((SUBLANES, D_RNN), f32), pltpu.VMEM((SUBLANES, D_RNN), f32)],
        compiler_params=_cparams(("arbitrary", "arbitrary")),
        name="lru_prompt",
    )(x, mod, g, w_in, b_in, w_conv, b_conv, wg, bg, lam, w_out)


def _lru_sample_body(x_ref, mod_ref, g_ref, h0_ref, taps_ref, win_ref, bin_ref, wc_ref, bc_ref, wg_ref, bg_ref, lam_ref,
                     wout_ref, xo_ref, hnew_ref, up_ref):
    x = x_ref[...]
    g = g_ref[...]
    h = _rms(x, g[0:1]) * (1.0 + mod_ref[1]) + mod_ref[0]
    proj = _mm(h, win_ref[...]) + bin_ref[...]
    y = _gelu(proj[:, :D_RNN])
    up = proj[:, D_RNN:]
    up_ref[...] = up
    wc = wc_ref[...]
    u = bc_ref[...] + wc[LRU_CONV_W - 1:LRU_CONV_W] * up
    for k in range(LRU_CONV_W - 1):
        u = u + wc[k:k + 1] * taps_ref[k]
    a, bx = _lru_gates(u, wg_ref, bg_ref[...], lam_ref[...])
    hs = a * h0_ref[...] + bx
    hnew_ref[...] = hs
    out = _mm(hs * y, wout_ref[...])
    xo_ref[...] = x + mod_ref[2] * _rms(out, g[1:2])


def _lru_sample(x, mod, g, h0, taps, w_in, b_in, w_conv, b_conv, wg, bg, lam, w_out):
    m, d = x.shape
    return pl.pallas_call(
        _lru_sample_body,
        out_shape=[jax.ShapeDtypeStruct((m, d), f32), jax.ShapeDtypeStruct((m, D_RNN), f32),
                   jax.ShapeDtypeStruct((m, D_RNN), f32)],
        compiler_params=_cparams(),
        name="lru_sample",
    )(x, mod, g, h0, taps, w_in, b_in, w_conv, b_conv, wg, bg, lam, w_out)


def _ffn_core(h, tap_fn, wi_ref, wc, bc, wo_ref, store_fn):
    acc = jnp.zeros((h.shape[0], D_MODEL), f32)
    hb = h.astype(bf16)
    for c in range(D_FF // FF_CHUNK):
        zs = []
        for base in (0, D_FF):
            lo = base + c * FF_CHUNK
            z = jnp.dot(hb, wi_ref[:, lo:lo + FF_CHUNK], preferred_element_type=f32)
            store_fn(z, lo)
            zc = bc[:, lo:lo + FF_CHUNK] + wc[FFN_CONV_W - 1:FFN_CONV_W, lo:lo + FF_CHUNK] * z
            for k in range(1, FFN_CONV_W):
                zc = zc + wc[FFN_CONV_W - 1 - k:FFN_CONV_W - k, lo:lo + FF_CHUNK] * tap_fn(z, lo, k)
            zs.append(zc)
        gz = _gelu(zs[0]) * zs[1]
        acc = acc + jnp.dot(gz.astype(bf16), wo_ref[c * FF_CHUNK:(c + 1) * FF_CHUNK, :], preferred_element_type=f32)
    return acc


def _ffn_prompt_body(has_pre, *refs):
    if has_pre:
        (x_ref, o_ref, wpre_ref, mod_ref, g_ref, wi_ref, wc_ref, bc_ref, wo_ref, xo_ref, fbuf_ref, tail_ref) = refs
    else:
        (x_ref, mod_ref, g_ref, wi_ref, wc_ref, bc_ref, wo_ref, xo_ref, fbuf_ref, tail_ref) = refs
    t = pl.program_id(1)

    @pl.when(t == 0)
    def _():
        tail_ref[...] = jnp.zeros(tail_ref.shape, f32)

    x = x_ref[0]
    tm = x.shape[0]
    mod = mod_ref[0]
    g = g_ref[...]
    if has_pre:
        x = x + mod[2:3] * _rms(jnp.dot(o_ref[0], wpre_ref[...], preferred_element_type=f32), g[1:2])
    h = _rms(x, g[2:3]) * (1.0 + mod[4:5]) + mod[3:4]

    def tap(z, lo, k):
        return _shift_rows(z, tail_ref[:, lo:lo + FF_CHUNK], k)

    def store(z, lo):
        fbuf_ref[0, :, lo:lo + FF_CHUNK] = z[tm - SUBLANES:tm]

    acc = _ffn_core(h, tap, wi_ref, wc_ref[...], bc_ref[...], wo_ref, store)
    tail_ref[...] = fbuf_ref[0]
    xo_ref[0] = x + mod[5:6] * _rms(acc, g[3:4])


def _ffn_prompt(x, mod, g, w_in, w_conv, b_conv, w_out, pre=None):
    bsz, t, d = x.shape
    tm = min(ROW_TILE, t)
    row_spec = pl.BlockSpec((1, tm, d), lambda b, i: (b, i, 0))
    small = lambda r, c: pl.BlockSpec((1, r, c), lambda b, i: (b, 0, 0))
    ins, specs = [x], [row_spec]
    if pre is not None:
        o_att, w_pre = pre
        ins += [o_att, w_pre]
        specs += [pl.BlockSpec((1, tm, o_att.shape[2]), lambda b, i: (b, i, 0)), _const_spec(w_pre.shape)]
    ins += [mod, g, w_in, w_conv, b_conv, w_out]
    specs += [small(N_MOD, d), _const_spec(g.shape), _const_spec(w_in.shape), _const_spec(w_conv.shape),
              _const_spec(b_conv.shape), _const_spec(w_out.shape)]
    return pl.pallas_call(
        functools.partial(_ffn_prompt_body, pre is not None),
        grid=(bsz, t // tm),
        in_specs=specs,
        out_specs=[row_spec, small(SUBLANES, 2 * D_FF)],
        out_shape=[jax.ShapeDtypeStruct((bsz, t, d), f32), jax.ShapeDtypeStruct((bsz, SUBLANES, 2 * D_FF), f32)],
        scratch_shapes=[pltpu.VMEM((SUBLANES, 2 * D_FF), f32)],
        compiler_params=_cparams(("arbitrary", "arbitrary")),
        name="ffn_prompt",
    )(*ins)


def _ffn_sample_body(has_pre, *refs):
    if has_pre:
        (x_ref, o_ref, wpre_ref, mod_ref, g_ref, taps_ref, wi_ref, wc_ref, bc_ref, wo_ref, xo_ref, z_ref) = refs
    else:
        (x_ref, mod_ref, g_ref, taps_ref, wi_ref, wc_ref, bc_ref, wo_ref, xo_ref, z_ref) = refs
    x = x_ref[...]
    g = g_ref[...]
    if has_pre:
        x = x + mod_ref[2] * _rms(jnp.dot(o_ref[...], wpre_ref[...], preferred_element_type=f32), g[1:2])
    h = _rms(x, g[2:3]) * (1.0 + mod_ref[4]) + mod_ref[3]

    def tap(z, lo, k):
        return taps_ref[FFN_CONV_W - 1 - k, :, lo:lo + FF_CHUNK]

    def store(z, lo):
        z_ref[:, lo:lo + FF_CHUNK] = z

    acc = _ffn_core(h, tap, wi_ref, wc_ref[...], bc_ref[...], wo_ref, store)
    xo_ref[...] = x + mod_ref[5] * _rms(acc, g[3:4])


def _ffn_sample(x, mod, g, taps, w_in, w_conv, b_conv, w_out, pre=None):
    m, d = x.shape
    ins = [x] + (list(pre) if pre is not None else []) + [mod, g, taps, w_in, w_conv, b_conv, w_out]
    return pl.pallas_call(
        functools.partial(_ffn_sample_body, pre is not None),
        out_shape=[jax.ShapeDtypeStruct((m, d), f32), jax.ShapeDtypeStruct((m, 2 * D_FF), f32)],
        compiler_params=_cparams(),
        name="ffn_sample",
    )(*ins)


def _dup_heads(col):
    lane = lax.broadcasted_iota(jnp.int32, col.shape, 1)
    rc = pltpu.roll(col, HEAD_DIM, 1)
    return jnp.where(lane < HEAD_DIM, col, rc), jnp.where(lane < HEAD_DIM, rc, col)


def _kv_prompt_body(x_ref, mod_ref, g_ref, w_ref, cos_ref, sin_ref, rows_ref, win_ref, ksd_ref, vsd_ref, kwd_ref, vwd_ref):
    x = x_ref[0]
    mod = mod_ref[0]
    h = _rms(x, g_ref[...]) * (1.0 + mod[1:2]) + mod[0:1]
    kv = _mm(h, w_ref[...])
    cos_t, sin_t = cos_ref[...], sin_ref[...]
    w = KV_WIDTH
    k_sel = _rope_wide(kv[:, 2 * w:3 * w], cos_t, sin_t)
    v_sel = kv[:, 3 * w:4 * w]
    k_win = _rope_wide(kv[:, 4 * w:5 * w], cos_t, sin_t)
    v_win = kv[:, 5 * w:6 * w]
    rows_ref[0, :, 0:2 * w] = kv[:, 0:2 * w]
    rows_ref[0, :, 2 * w:3 * w] = k_sel
    rows_ref[0, :, 3 * w:4 * w] = v_sel
    win_ref[0, :, 0:w] = k_win
    win_ref[0, :, w:2 * w] = v_win
    for src, dst in ((k_sel, ksd_ref), (v_sel, vsd_ref), (k_win, kwd_ref), (v_win, vwd_ref)):
        for j in range(w // LANES):
            even, odd = _dup_heads(src[:, j * LANES:(j + 1) * LANES])
            dst[0, 2 * j] = even.astype(bf16)
            dst[0, 2 * j + 1] = odd.astype(bf16)


def _kv_prompt(x, mod, g_kv, w_kv, cos_t, sin_t):
    bsz, t, d = x.shape
    tm = min(ROW_TILE, t)
    row = lambda c: pl.BlockSpec((1, tm, c), lambda b, i: (b, i, 0))
    dup = pl.BlockSpec((1, N_KV_HEADS, tm, LANES), lambda b, i: (b, 0, i, 0))
    dup_shape = jax.ShapeDtypeStruct((bsz, N_KV_HEADS, t, LANES), bf16)
    return pl.pallas_call(
        _kv_prompt_body,
        grid=(bsz, t // tm),
        in_specs=[row(d), pl.BlockSpec((1, 2, d), lambda b, i: (b, 0, 0)), _const_spec(g_kv.shape), _const_spec(w_kv.shape),
                  pl.BlockSpec((tm, LANES), lambda b, i: (i, 0)), pl.BlockSpec((tm, LANES), lambda b, i: (i, 0))],
        out_specs=[row(4 * KV_WIDTH), row(2 * KV_WIDTH), dup, dup, dup, dup],
        out_shape=[jax.ShapeDtypeStruct((bsz, t, 4 * KV_WIDTH), f32), jax.ShapeDtypeStruct((bsz, t, 2 * KV_WIDTH), f32),
                   dup_shape, dup_shape, dup_shape, dup_shape],
        compiler_params=_cparams(("arbitrary", "arbitrary")),
        name="kv_prompt",
    )(x, mod, g_kv, w_kv, cos_t, sin_t)


def _kv_sample_body(x_ref, mod_ref, g_ref, w_ref, cos_ref, sin_ref, rows_ref, win_ref):
    h = _rms(x_ref[...], g_ref[...]) * (1.0 + mod_ref[1]) + mod_ref[0]
    kv = _mm(h, w_ref[...])
    cos_t, sin_t = cos_ref[...], sin_ref[...]
    w = KV_WIDTH
    rows_ref[:, 0:2 * w] = kv[:, 0:2 * w]
    rows_ref[:, 2 * w:3 * w] = _rope_wide(kv[:, 2 * w:3 * w], cos_t, sin_t)
    rows_ref[:, 3 * w:4 * w] = kv[:, 3 * w:4 * w]
    win_ref[:, 0:w] = _rope_wide(kv[:, 4 * w:5 * w], cos_t, sin_t)
    win_ref[:, w:2 * w] = kv[:, 5 * w:6 * w]


def _kv_sample(x, mod, g_kv, w_kv, cos_t, sin_t):
    m = x.shape[0]
    return pl.pallas_call(
        _kv_sample_body,
        out_shape=[jax.ShapeDtypeStruct((m, 4 * KV_WIDTH), f32), jax.ShapeDtypeStruct((m, 2 * KV_WIDTH), f32)],
        compiler_params=_cparams(),
        name="kv_sample",
    )(x, mod, g_kv, w_kv, cos_t, sin_t)


def _compress_body(per_group, *refs):
    n_slab = 2 * KV_WIDTH // LANES
    slabs = refs[:n_slab]
    pe_ref, wt_ref, wb_ref, b1_ref, w2_ref = refs[n_slab:n_slab + 5]
    outs = refs[n_slab + 5:]
    n_chunk = slabs[0].shape[1] // D_CMP
    per_slot = KV_WIDTH // LANES
    for s in range(2):
        top = jnp.zeros((n_chunk, N_KV_HEADS * CMP_HIDDEN), f32)
        bot = jnp.zeros((n_chunk, N_KV_HEADS * CMP_HIDDEN), f32)
        for r in range(D_CMP):
            xr = jnp.concatenate([slabs[s * per_slot + j][0, pl.ds(r, n_chunk, stride=D_CMP), :] for j in range(per_slot)], axis=1)
            top = top + _mm(xr + pe_ref[s, r:r + 1], wt_ref[s, r])
            bot = bot + _mm(xr + pe_ref[s, D_CMP + r:D_CMP + r + 1], wb_ref[s, r])
        hidden = top + pltpu.roll(bot, n_chunk - 1, 0) + b1_ref[s]
        act = _gelu(hidden).astype(bf16)
        if per_group:
            for gi in range(N_KV_HEADS):
                outs[s][0, gi] = jnp.dot(act[:, gi * CMP_HIDDEN:(gi + 1) * CMP_HIDDEN], w2_ref[s, gi],
                                         preferred_element_type=f32).astype(bf16)
        else:
            tok = jnp.zeros((n_chunk, KV_WIDTH), f32)
            for gi in range(N_KV_HEADS):
                tok = tok + jnp.dot(act[:, gi * CMP_HIDDEN:(gi + 1) * CMP_HIDDEN], w2_ref[s, gi], preferred_element_type=f32)
            outs[s][0] = tok.astype(bf16)


def _compress(rows, cw, per_group):
    bsz, t = rows.shape[:2]
    n_chunk = t // D_CMP
    n_slab = 2 * KV_WIDTH // LANES
    slab_specs = [pl.BlockSpec((1, t, LANES), functools.partial(lambda j, b: (b, 0, j), j)) for j in range(n_slab)]
    pe, wt, wb, b1, w2 = cw
    if per_group:
        out_spec = pl.BlockSpec((1, N_KV_HEADS, n_chunk, LANES), lambda b: (b, 0, 0, 0))
        out_shape = jax.ShapeDtypeStruct((bsz, N_KV_HEADS, n_chunk, LANES), bf16)
    else:
        out_spec = pl.BlockSpec((1, n_chunk, KV_WIDTH), lambda b: (b, 0, 0))
        out_shape = jax.ShapeDtypeStruct((bsz, n_chunk, KV_WIDTH), bf16)
    return pl.pallas_call(
        functools.partial(_compress_body, per_group),
        grid=(bsz,),
        in_specs=slab_specs + [_const_spec(pe.shape), _const_spec(wt.shape), _const_spec(wb.shape), _const_spec(b1.shape),
                               _const_spec(w2.shape)],
        out_specs=[out_spec, out_spec],
        out_shape=[out_shape, out_shape],
        compiler_params=_cparams(("arbitrary",)),
        name="compress_prompt" if per_group else "compress_sample",
    )(*([rows] * n_slab), pe, wt, wb, b1, w2)


def _q_prompt_body(x_ref, mod_ref, g_ref, wq_ref, wgt_ref, cos_ref, sin_ref, qraw_ref, qrot_ref, gate_ref):
    mod = mod_ref[0]
    h = _rms(x_ref[0], g_ref[0:1]) * (1.0 + mod[1:2]) + mod[0:1]
    hb = h.astype(bf16)
    q = jnp.dot(hb, wq_ref[...], preferred_element_type=f32)
    gate_ref[0] = jax.nn.sigmoid(jnp.dot(hb, wgt_ref[...], preferred_element_type=f32))
    scale = HEAD_DIM ** -0.5
    qraw_ref[0] = (q * scale).astype(bf16)
    qrot_ref[0] = (_rope_wide(q, cos_ref[...], sin_ref[...]) * scale).astype(bf16)


def _q_prompt(x, mod, g, wq, wgt, cos_t, sin_t):
    bsz, t, d = x.shape
    tm = min(ROW_TILE, t)
    row = lambda c: pl.BlockSpec((1, tm, c), lambda b, i: (b, i, 0))
    return pl.pallas_call(
        _q_prompt_body,
        grid=(bsz, t // tm),
        in_specs=[row(d), pl.BlockSpec((1, N_MOD, d), lambda b, i: (b, 0, 0)), _const_spec(g.shape), _const_spec(wq.shape),
                  _const_spec(wgt.shape), pl.BlockSpec((tm, LANES), lambda b, i: (i, 0)),
                  pl.BlockSpec((tm, LANES), lambda b, i: (i, 0))],
        out_specs=[row(Q_WIDTH), row(Q_WIDTH), row(N_KV_HEADS * LANES)],
        out_shape=[jax.ShapeDtypeStruct((bsz, t, Q_WIDTH), bf16), jax.ShapeDtypeStruct((bsz, t, Q_WIDTH), bf16),
                   jax.ShapeDtypeStruct((bsz, t, N_KV_HEADS * LANES), f32)],
        compiler_params=_cparams(("arbitrary", "arbitrary")),
        name="q_prompt",
    )(x, mod, g, wq, wgt, cos_t, sin_t)


def _q_sample_body(x_ref, mod_ref, g_ref, wq_ref, wgt_ref, cos_ref, sin_ref, qraw_ref, qrot_ref, gate_ref):
    h = _rms(x_ref[...], g_ref[0:1]) * (1.0 + mod_ref[1]) + mod_ref[0]
    hb = h.astype(bf16)
    q = jnp.dot(hb, wq_ref[...], preferred_element_type=f32)
    gate_ref[...] = jax.nn.sigmoid(jnp.dot(hb, wgt_ref[...], preferred_element_type=f32))
    scale = HEAD_DIM ** -0.5
    qraw_ref[...] = (q * scale).astype(bf16)
    qrot_ref[...] = (_rope_wide(q, cos_ref[...], sin_ref[...]) * scale).astype(bf16)


def _q_sample(x, mod, g, wq, wgt, cos_t, sin_t):
    m = x.shape[0]
    return pl.pallas_call(
        _q_sample_body,
        out_shape=[jax.ShapeDtypeStruct((m, wq.shape[1]), bf16), jax.ShapeDtypeStruct((m, wq.shape[1]), bf16),
                   jax.ShapeDtypeStruct((m, wgt.shape[1]), f32)],
        compiler_params=_cparams(),
        name="q_sample",
    )(x, mod, g, wq, wgt, cos_t, sin_t)


def _stack_heads(q):
    lane = lax.broadcasted_iota(jnp.int32, (q.shape[0], LANES), 1)
    zero = jnp.zeros((q.shape[0], LANES), q.dtype)
    parts = []
    for r in range(GROUP):
        col = q[:, (r // 2) * LANES:(r // 2 + 1) * LANES]
        parts.append(jnp.where((lane < HEAD_DIM) == (r % 2 == 0), col, zero))
    return jnp.concatenate(parts, axis=0)


def _attn_prompt_body(n_cmp, n_sel, qraw_ref, qrot_ref, gate_ref, kc_ref, vc_ref, c2s_ref, ks_ref, vs_ref, kw_ref, vw_ref,
                      o_ref, sel_ref, oc_ref, ms_ref, ls_ref, as_ref, mw_ref, lw_ref, aw_ref):
    qi = pl.program_id(2)
    kc = pl.program_id(3)
    n_kc = pl.num_programs(3)
    tq = qraw_ref.shape[1]
    rows = GROUP * tq
    q_chunk = (qi * tq) // K_CHUNK

    def q_pos(shape):
        r = lax.broadcasted_iota(jnp.int32, shape, 0)
        return qi * tq + r % tq

    @pl.when(kc == 0)
    def _():
        q4 = _stack_heads(qraw_ref[0])
        s = _mm_nt(q4, kc_ref[0, 0])
        n_pad = s.shape[1]
        blk = lax.broadcasted_iota(jnp.int32, s.shape, 1)
        mask = (blk * D_CMP + (L_CMP - 1) <= q_pos(s.shape)) & (blk < n_cmp)
        s = jnp.where(mask, s, NEG_INF)
        p = jnp.where(mask, jnp.exp(s - jnp.max(s, axis=1, keepdims=True)), 0.0)
        l = jnp.sum(p, axis=1, keepdims=True)
        p = p / jnp.where(l > 0.0, l, 1.0)
        oc_ref[...] = jnp.dot(p.astype(bf16), vc_ref[0, 0], preferred_element_type=f32)
        p_group = p[0:tq]
        for r in range(1, GROUP):
            p_group = p_group + p[r * tq:(r + 1) * tq]
        imp = _mm_exact_rhs(p_group, c2s_ref[...])
        cur = q_pos((tq, 1)) // L_SEL
        sel_ref[...] = _topk_mask(imp, cur, n_sel)
        for m_ref, l_ref, a_ref in ((ms_ref, ls_ref, as_ref), (mw_ref, lw_ref, aw_ref)):
            m_ref[...] = jnp.full(m_ref.shape, NEG_INF, f32)
            l_ref[...] = jnp.zeros(l_ref.shape, f32)
            a_ref[...] = jnp.zeros(a_ref.shape, f32)

    def key_pos(shape):
        return kc * K_CHUNK + lax.broadcasted_iota(jnp.int32, shape, 1)

    @pl.when(kc <= q_chunk)
    def _():
        q4 = _stack_heads(qrot_ref[0])
        s = _mm_nt(q4, ks_ref[0, 0])
        jj = lax.broadcasted_iota(jnp.int32, (LANES, K_CHUNK), 0)
        kk = lax.broadcasted_iota(jnp.int32, (LANES, K_CHUNK), 1)
        expand = (jj == kc * (K_CHUNK // L_SEL) + kk // L_SEL).astype(bf16)
        picked = jnp.dot(sel_ref[...].astype(bf16), expand, preferred_element_type=f32) > 0.5
        picked = jnp.concatenate([picked] * GROUP, axis=0)
        mask = picked & (key_pos(s.shape) <= q_pos(s.shape))
        ms_ref[...], ls_ref[...], as_ref[...] = _softmax_step(s, mask, ms_ref[...], ls_ref[...], as_ref[...], vs_ref[0, 0])

    @pl.when((kc <= q_chunk) & (kc * K_CHUNK + K_CHUNK > qi * tq - WINDOW + 1))
    def _():
        q4 = _stack_heads(qrot_ref[0])
        s = _mm_nt(q4, kw_ref[0, 0])
        dpos = q_pos(s.shape) - key_pos(s.shape)
        mask = (dpos >= 0) & (dpos < WINDOW)
        mw_ref[...], lw_ref[...], aw_ref[...] = _softmax_step(s, mask, mw_ref[...], lw_ref[...], aw_ref[...], vw_ref[0, 0])

    @pl.when(kc == n_kc - 1)
    def _():
        gate = gate_ref[0]
        o_s = as_ref[...] / ls_ref[...]
        o_w = aw_ref[...] / lw_ref[...]
        o_c = oc_ref[...]
        lane = lax.broadcasted_iota(jnp.int32, (tq, LANES), 1)
        heads = []
        for r in range(GROUP):
            sl = slice(r * tq, (r + 1) * tq)
            heads.append(gate[:, 3 * r:3 * r + 1] * o_c[sl] + gate[:, 3 * r + 1:3 * r + 2] * o_s[sl]
                         + gate[:, 3 * r + 2:3 * r + 3] * o_w[sl])
        pairs = [jnp.where(lane < HEAD_DIM, heads[2 * p], heads[2 * p + 1]) for p in range(GROUP // 2)]
        o_ref[0] = jnp.concatenate(pairs, axis=1).astype(o_ref.dtype)


def _attn_prompt(q_raw, q_rot, gates, kcd, vcd, c2s, ksd, vsd, kwd, vwd, n_cmp, n_sel):
    bsz, t, _ = q_raw.shape
    tq = min(Q_TILE, t)
    n_kc = t // K_CHUNK
    gw = GROUP * HEAD_DIM
    q_spec = pl.BlockSpec((1, tq, gw), lambda b, g, i, k: (b, i, g))
    cmp_spec = pl.BlockSpec((1, 1) + kcd.shape[2:], lambda b, g, i, k: (b, g, 0, 0))
    kv_spec = pl.BlockSpec((1, 1, K_CHUNK, LANES), lambda b, g, i, k: (b, g, jnp.minimum(k, (i * tq) // K_CHUNK), 0))
    rows = GROUP * tq
    return pl.pallas_call(
        functools.partial(_attn_prompt_body, n_cmp, n_sel),
        grid=(bsz, N_KV_HEADS, t // tq, n_kc),
        in_specs=[q_spec, q_spec, pl.BlockSpec((1, tq, LANES), lambda b, g, i, k: (b, i, g)), cmp_spec, cmp_spec,
                  pl.BlockSpec(c2s.shape, lambda b, g, i, k: (0, 0)), kv_spec, kv_spec, kv_spec, kv_spec],
        out_specs=q_spec,
        out_shape=jax.ShapeDtypeStruct((bsz, t, Q_WIDTH), bf16),
        scratch_shapes=[pltpu.VMEM((tq, LANES), f32), pltpu.VMEM((rows, LANES), f32),
                        pltpu.VMEM((rows, 1), f32), pltpu.VMEM((rows, 1), f32), pltpu.VMEM((rows, LANES), f32),
                        pltpu.VMEM((rows, 1), f32), pltpu.VMEM((rows, 1), f32), pltpu.VMEM((rows, LANES), f32)],
        compiler_params=_cparams(("arbitrary", "arbitrary", "arbitrary", "arbitrary")),
        name="attn_prompt",
    )(q_raw, q_rot, gates, kcd, vcd, c2s, ksd, vsd, kwd, vwd)


def _attn_sample_body(n_cmp, n_sel, win_buf, qraw_ref, qrot_ref, gate_ref, kc_ref, vc_ref, c2s_ref, past_ref, new_ref,
                      wstate_ref, wnew_ref, o_ref):
    w = KV_WIDTH
    q_raw = qraw_ref[0]
    q_rot = qrot_ref[0]
    n_head = q_raw.shape[0]
    head = lax.broadcasted_iota(jnp.int32, (n_head, n_head), 0)
    head2 = lax.broadcasted_iota(jnp.int32, (n_head, n_head), 1)
    same_group = (head // GROUP == head2 // GROUP).astype(bf16)

    s = _mm_nt(q_raw, kc_ref[0])
    blk = lax.broadcasted_iota(jnp.int32, s.shape, 1)
    mask = blk < n_cmp
    s = jnp.where(mask, s, NEG_INF)
    p = jnp.where(mask, jnp.exp(s - jnp.max(s, axis=1, keepdims=True)), 0.0)
    p = p / jnp.sum(p, axis=1, keepdims=True)
    o_c = jnp.dot(p.astype(bf16), vc_ref[0], preferred_element_type=f32)
    imp = _mm_exact_rhs(_mm_exact_lhs(same_group, p), c2s_ref[...])
    cur = jnp.full((n_head, 1), n_sel - 1, jnp.int32)
    sel = _topk_mask(imp, cur, n_sel)

    def attend(q, k_past, v_past, mask_past, k_new, v_new, new_ok):
        sp = jnp.where(mask_past, _mm_nt(q, k_past.astype(bf16)), NEG_INF)
        sn = jnp.sum(q.astype(f32) * k_new.astype(bf16).astype(f32), axis=1, keepdims=True)
        sn = jnp.where(new_ok, sn, NEG_INF)
        m = jnp.maximum(jnp.max(sp, axis=1, keepdims=True), sn)
        pp = jnp.where(mask_past, jnp.exp(sp - m), 0.0)
        pn = jnp.where(new_ok, jnp.exp(sn - m), 0.0)
        l = jnp.sum(pp, axis=1, keepdims=True) + pn
        acc = jnp.dot(pp.astype(bf16), v_past.astype(bf16), preferred_element_type=f32)
        acc = acc + pn.astype(bf16).astype(f32) * v_new.astype(bf16).astype(f32)
        return acc / l

    n_past = past_ref.shape[1]
    jj = lax.broadcasted_iota(jnp.int32, (LANES, n_past), 0)
    kk = lax.broadcasted_iota(jnp.int32, (LANES, n_past), 1)
    expand = (jj == kk // L_SEL).astype(bf16)
    picked = jnp.dot(sel.astype(bf16), expand, preferred_element_type=f32) > 0.5
    new_row = new_ref[0]
    o_s = attend(q_rot, past_ref[0, :, 0:w], past_ref[0, :, w:2 * w], picked, new_row[:, 2 * w:3 * w], new_row[:, 3 * w:4 * w],
                 sel[:, n_sel - 1:n_sel] > 0.5)

    idx = lax.broadcasted_iota(jnp.int32, (n_head, win_buf), 1)
    in_window = win_buf - idx < WINDOW
    wnew = wnew_ref[0]
    o_w = attend(q_rot, wstate_ref[0, :, 0:w], wstate_ref[0, :, w:2 * w], in_window, wnew[:, 0:w], wnew[:, w:2 * w],
                 jnp.full((n_head, 1), True))

    gate = gate_ref[0]
    o = gate[:, 0:1] * o_c + gate[:, 1:2] * o_s + gate[:, 2:3] * o_w
    hrow = lax.broadcasted_iota(jnp.int32, o.shape, 0)
    lane = lax.broadcasted_iota(jnp.int32, o.shape, 1)
    o_ref[0] = jnp.where(lane // HEAD_DIM == hrow // GROUP, o, 0.0).astype(o_ref.dtype)


def _attn_sample(q_raw, q_rot, gates, kc, vc, c2s, past, new_rows, wstate, wnew, n_cmp, n_sel):
    bsz = q_raw.shape[0]
    n_past = past.shape[1]
    win_buf = wstate.shape[1]
    per_seq = lambda a: pl.BlockSpec((1,) + a.shape[1:], lambda b: (b,) + (0,) * (a.ndim - 1))
    return pl.pallas_call(
        functools.partial(_attn_sample_body, n_cmp, n_sel, win_buf),
        grid=(bsz,),
        in_specs=[per_seq(q_raw), per_seq(q_rot), per_seq(gates), per_seq(kc), per_seq(vc),
                  pl.BlockSpec(c2s.shape, lambda b: (0, 0)),
                  pl.BlockSpec((1, n_past, 2 * KV_WIDTH), lambda b: (b, 0, 1)), per_seq(new_rows), per_seq(wstate), per_seq(wnew)],
        out_specs=pl.BlockSpec((1, N_HEADS, KV_WIDTH), lambda b: (b, 0, 0)),
        out_shape=jax.ShapeDtypeStruct((bsz, N_HEADS, KV_WIDTH), bf16),
        compiler_params=_cparams(("arbitrary",)),
        name="attn_sample",
    )(q_raw, q_rot, gates, kc, vc, c2s, past, new_rows, wstate, wnew)


def _rope_tables(pos):
    half = ROT_DIM // 2
    inv = np.power(ROPE_THETA, -np.arange(half, dtype=np.float64) * (2.0 / ROT_DIM))
    ang = np.asarray(pos, np.float64)[:, None] * inv[None, :]
    cos_h = np.ones((len(pos), HEAD_DIM), np.float32)
    sin_h = np.zeros((len(pos), HEAD_DIM), np.float32)
    cos_h[:, :half] = np.cos(ang)
    cos_h[:, half:ROT_DIM] = np.cos(ang)
    sin_h[:, :half] = -np.sin(ang)
    sin_h[:, half:ROT_DIM] = np.sin(ang)
    return jnp.asarray(np.tile(cos_h, (1, 2))), jnp.asarray(np.tile(sin_h, (1, 2)))


def _cmp_to_sel(t, n_rows):
    nc = (t - L_CMP) // D_CMP + 1
    ns = -(-t // L_SEL)
    c_start = np.arange(nc) * D_CMP
    s_start = np.arange(ns) * L_SEL
    m = np.zeros((n_rows, LANES), np.float32)
    m[:nc, :ns] = (c_start[:, None] < s_start[None, :] + L_SEL) & (c_start[:, None] + L_CMP > s_start[None, :])
    return jnp.asarray(m, bf16), nc, ns


def _gate_blockdiag(w_gate):
    wg = w_gate.astype(bf16).reshape(2, LRU_HEADS // 2, 2, LRU_BLOCK, LRU_BLOCK)
    z = jnp.zeros_like(wg[:, :, 0])
    top = jnp.concatenate([wg[:, :, 0], z], axis=-1)
    bot = jnp.concatenate([z, wg[:, :, 1]], axis=-1)
    return jnp.concatenate([top, bot], axis=-2)


def _compress_weights(pe_cmp, w_cmp1, b_cmp1, w_cmp2, per_group):
    pe = jnp.tile(pe_cmp, (1, 1, N_KV_HEADS))
    w1 = w_cmp1.astype(bf16).reshape(2, 2, D_CMP, HEAD_DIM, CMP_HIDDEN)
    eye = jnp.eye(N_KV_HEADS, dtype=bf16)
    bd = jnp.einsum('shrdk,gq->shrgdqk', w1, eye).reshape(2, 2, D_CMP, KV_WIDTH, N_KV_HEADS * CMP_HIDDEN)
    b1 = jnp.tile(b_cmp1, (1, N_KV_HEADS)).reshape(2, 1, N_KV_HEADS * CMP_HIDDEN)
    w2 = w_cmp2.astype(bf16)
    if per_group:
        w2x = jnp.broadcast_to(jnp.concatenate([w2, w2], axis=-1)[:, None], (2, N_KV_HEADS, CMP_HIDDEN, LANES))
    else:
        w2x = jnp.einsum('skd,gq->sgkqd', w2, eye).reshape(2, N_KV_HEADS, CMP_HIDDEN, KV_WIDTH)
    return pe, bd[:, 0], bd[:, 1], b1, w2x


def _q_weights_prompt(w_qg):
    wq = w_qg[:, :Q_WIDTH].astype(bf16)
    wg = w_qg[:, Q_WIDTH:].reshape(D_MODEL, N_KV_HEADS, 3 * GROUP)
    wg = jnp.pad(wg, ((0, 0), (0, 0), (0, LANES - 3 * GROUP))).reshape(D_MODEL, N_KV_HEADS * LANES).astype(bf16)
    return wq, wg


def _q_weights_sample(w_qg):
    wq = w_qg[:, :Q_WIDTH].astype(bf16).reshape(D_MODEL, N_KV_HEADS, GROUP, HEAD_DIM)
    eye = jnp.eye(N_KV_HEADS, dtype=bf16)
    wq = jnp.einsum('kgrd,gq->kgrqd', wq, eye).reshape(D_MODEL, N_HEADS * KV_WIDTH)
    wg = w_qg[:, Q_WIDTH:].reshape(D_MODEL, N_HEADS, 3)
    wg = jnp.pad(wg, ((0, 0), (0, 0), (0, LANES - 3))).reshape(D_MODEL, N_HEADS * LANES).astype(bf16)
    return wq, wg


def _out_weights_sample(w_o):
    w = w_o.astype(bf16).reshape(N_KV_HEADS, GROUP, HEAD_DIM, D_MODEL)
    eye = jnp.eye(N_KV_HEADS, dtype=bf16)
    return jnp.einsum('grdn,gq->grqdn', w, eye).reshape(N_HEADS * KV_WIDTH, D_MODEL)


def kernel(x_prompt, x_sample, c_prompt, c_sample, cache_nsa_kv, page_table, state_nsa_win, state_lru_h, state_lru_conv, state_ffn_conv, w_mod, b_mod, g_norm, w_lru_in, b_lru_in, w_lru_conv, b_lru_conv, w_lru_gate, b_lru_gate, lru_lambda, w_lru_out, g_kv, w_mod_kv, b_mod_kv, w_kv, pe_cmp, w_cmp1, b_cmp1, w_cmp2, w_nsa_qg, w_nsa_out, w_ffn_in, w_ffn_conv, b_ffn_conv, w_ffn_out):
    bp, tp, d = x_prompt.shape
    bs, ts, _ = x_sample.shape
    assert ts == 1 and d == D_MODEL
    depth = w_mod.shape[0]
    n_a = w_lru_in.shape[0]
    past_len = page_table.shape[1] * PAGE_SIZE
    win_buf = state_nsa_win.shape[1]

    c_all = jnp.concatenate([c_prompt, c_sample], axis=0)
    mod_all = _ada_mod(c_all, w_mod, b_mod).reshape(depth, bp + bs, N_MOD, d)
    mod_p = mod_all[:, :bp]
    mod_s = jnp.swapaxes(mod_all[:, bp:], 1, 2)
    modkv_all = _ada_mod(c_all, w_mod_kv[None], b_mod_kv[None]).reshape(bp + bs, 2, d)
    modkv_p = modkv_all[:bp]
    modkv_s = jnp.swapaxes(modkv_all[bp:], 0, 1)

    cos_p, sin_p = _rope_tables(np.arange(tp))
    cos_s, sin_s = _rope_tables(np.asarray([past_len]))

    x_p = x_prompt
    x_s = x_sample.reshape(bs, d)
    lru_h_p, lru_h_s, lru_c_p, lru_c_s, ffn_c_p, ffn_c_s = [], [], [], [], [], []
    row1 = lambda a: a.reshape(1, -1)

    ctx_p = ctx_s = None
    for l in range(depth):
        g = g_norm[l]
        wi = w_ffn_in[l].astype(bf16)
        wo = w_ffn_out[l].astype(bf16)
        ffn_taps_s = jnp.swapaxes(state_ffn_conv[l], 0, 1)
        pre_p = pre_s = None
        if l < n_a:
            w_in = w_lru_in[l].astype(bf16)
            w_out = w_lru_out[l].astype(bf16)
            wg = _gate_blockdiag(w_lru_gate[l])
            args = (w_in, row1(b_lru_in[l]), w_lru_conv[l], row1(b_lru_conv[l]), wg, b_lru_gate[l], row1(lru_lambda[l]), w_out)
            x_p, h_last, cbuf = _lru_prompt(x_p, mod_p[l], g, *args)
            lru_h_p.append(h_last[:, 0])
            lru_c_p.append(cbuf[:, SUBLANES - (LRU_CONV_W - 1):])
            taps = jnp.swapaxes(state_lru_conv[l], 0, 1)
            x_s, h_new, up_new = _lru_sample(x_s, mod_s[l], g, state_lru_h[l], taps, *args)
            lru_h_s.append(h_new)
            lru_c_s.append(jnp.concatenate([state_lru_conv[l][:, 1:], up_new[:, None]], axis=1))
        else:
            j = l - n_a
            wq_p, wgt_p = _q_weights_prompt(w_nsa_qg[j])
            q_raw, q_rot, gates = _q_prompt(x_p, mod_p[l], g, wq_p, wgt_p, cos_p, sin_p)
            o_att = _attn_prompt(q_raw, q_rot, gates, *ctx_p)
            pre_p = (o_att, w_nsa_out[j].astype(bf16))
            wq_s, wgt_s = _q_weights_sample(w_nsa_qg[j])
            q_raw_s, q_rot_s, gates_s = _q_sample(x_s, mod_s[l], g, wq_s, wgt_s, cos_s, sin_s)
            o_s = _attn_sample(q_raw_s.reshape(bs, N_HEADS, KV_WIDTH), q_rot_s.reshape(bs, N_HEADS, KV_WIDTH),
                               gates_s.reshape(bs, N_HEADS, LANES), *ctx_s)
            pre_s = (o_s.reshape(bs, N_HEADS * KV_WIDTH), _out_weights_sample(w_nsa_out[j]))
        x_p, fbuf = _ffn_prompt(x_p, mod_p[l], g, wi, w_ffn_conv[l], row1(b_ffn_conv[l]), wo, pre=pre_p)
        ffn_c_p.append(fbuf[:, SUBLANES - (FFN_CONV_W - 1):])
        x_s, z_new = _ffn_sample(x_s, mod_s[l], g, ffn_taps_s, wi, w_ffn_conv[l], row1(b_ffn_conv[l]), wo, pre=pre_s)
        ffn_c_s.append(jnp.concatenate([state_ffn_conv[l][:, 1:], z_new[:, None]], axis=1))

        if l == n_a - 1:
            w_kv_b = w_kv.astype(bf16)
            rows_p, win_p, ksd, vsd, kwd, vwd = _kv_prompt(x_p, modkv_p, row1(g_kv), w_kv_b, cos_p, sin_p)
            kcd, vcd = _compress(rows_p, _compress_weights(pe_cmp, w_cmp1, b_cmp1, w_cmp2, True), True)
            c2s_p, nc_p, ns_p = _cmp_to_sel(tp, kcd.shape[2])
            ctx_p = (kcd, vcd, c2s_p, ksd, vsd, kwd, vwd, nc_p, ns_p)
            rows_s, win_s = _kv_sample(x_s, modkv_s, row1(g_kv), w_kv_b, cos_s, sin_s)
            pages = cache_nsa_kv.reshape(cache_nsa_kv.shape[0], PAGE_SIZE, 4 * KV_WIDTH)
            past = pages[page_table].reshape(bs, past_len, 4 * KV_WIDTH)
            kc_s, vc_s = _compress(past, _compress_weights(pe_cmp, w_cmp1, b_cmp1, w_cmp2, False), False)
            c2s_s, nc_s, ns_s = _cmp_to_sel(past_len + ts, kc_s.shape[1])
            ctx_s = (kc_s, vc_s, c2s_s, past, rows_s.reshape(bs, 1, 4 * KV_WIDTH),
                     state_nsa_win.reshape(bs, win_buf, 2 * KV_WIDTH), win_s.reshape(bs, 1, 2 * KV_WIDTH), nc_s, ns_s)

    keep_p = min(WINDOW, tp)
    kv_rows_p = rows_p.reshape(bp, tp, 4, N_KV_HEADS, HEAD_DIM)
    win_out_p = win_p[:, tp - keep_p:].reshape(bp, keep_p, 2, N_KV_HEADS, HEAD_DIM)
    kv_rows_s = rows_s.reshape(bs, ts, 4, N_KV_HEADS, HEAD_DIM)
    keep_s = min(WINDOW, past_len + ts)
    win_all = jnp.concatenate([state_nsa_win, win_s.reshape(bs, ts, 2, N_KV_HEADS, HEAD_DIM)], axis=1)
    win_out_s = win_all[:, win_buf + ts - keep_s:]
    return (x_p, x_s.reshape(bs, ts, d), kv_rows_p, kv_rows_s, win_out_p, win_out_s,
            jnp.stack(lru_h_p), jnp.stack(lru_h_s), jnp.stack(lru_c_p), jnp.stack(lru_c_s),
            jnp.stack(ffn_c_p), jnp.stack(ffn_c_s))
```

```python
import functools

import numpy as np
import jax
import jax.numpy as jnp
from jax import lax
from jax.experimental import pallas as pl
from jax.experimental.pallas import tpu as pltpu

f32 = jnp.float32
bf16 = jnp.bfloat16

D_MODEL = 1024
D_RNN = D_MODEL
LRU_HEADS = 8
LRU_BLOCK = D_RNN // LRU_HEADS
LRU_CONV_W = 4
LRU_C = 8.0
N_HEADS = 16
HEAD_DIM = 64
N_KV_HEADS = 4
GROUP = N_HEADS // N_KV_HEADS
Q_WIDTH = N_HEADS * HEAD_DIM
KV_WIDTH = N_KV_HEADS * HEAD_DIM
L_CMP = 32
D_CMP = 16
CMP_HIDDEN = 128
L_SEL = 64
TOP_N = 8
WINDOW = 512
ROT_DIM = HEAD_DIM // 4
ROPE_THETA = 500000.0
D_FF = 3 * D_MODEL
FFN_CONV_W = 3
N_MOD = 6
EPS = 1e-6
NEG_INF = -1e30
FORCED_SCORE = 1e6
PAGE_SIZE = 128

LANES = 128
SUBLANES = 8
VMEM_LIMIT = 60 * 1024 * 1024
ROW_TILE = 256
FFN_TILE = 512
Q_TILE = 128
K_CHUNK = 512
FF_CHUNK = 512
BELOW_ALL = -3.0e38
SEL_GROUP = 16


def _cparams(sem=None):
    return pltpu.CompilerParams(dimension_semantics=sem, vmem_limit_bytes=VMEM_LIMIT)


def _const_spec(shape):
    nd = len(shape)
    return pl.BlockSpec(shape, lambda *_: (0,) * nd, pipeline_mode=pl.Buffered(1))


def _rms(x, g):
    return x * lax.rsqrt(jnp.mean(x * x, axis=-1, keepdims=True) + EPS) * g


def _gelu(x):
    return 0.5 * x * (1.0 + jnp.tanh(0.7978845608028654 * (x + 0.044715 * (x * x * x))))


def _mm(a, w):
    return jnp.dot(a.astype(bf16), w, preferred_element_type=f32)


def _mm_nt(a, b):
    return lax.dot_general(a, b, (((1,), (1,)), ((), ())), preferred_element_type=f32)


def _mm_exact_rhs(a, b01):
    a1 = a.astype(bf16)
    r1 = a - a1.astype(f32)
    a2 = r1.astype(bf16)
    a3 = (r1 - a2.astype(f32)).astype(bf16)
    return (jnp.dot(a1, b01, preferred_element_type=f32) + jnp.dot(a2, b01, preferred_element_type=f32)
            + jnp.dot(a3, b01, preferred_element_type=f32))


def _mm_exact_lhs(a01, b):
    b1 = b.astype(bf16)
    r1 = b - b1.astype(f32)
    b2 = r1.astype(bf16)
    b3 = (r1 - b2.astype(f32)).astype(bf16)
    return (jnp.dot(a01, b1, preferred_element_type=f32) + jnp.dot(a01, b2, preferred_element_type=f32)
            + jnp.dot(a01, b3, preferred_element_type=f32))


def _shift_rows(cur, tail, k):
    r = pltpu.roll(cur, k, 0)
    rp = pltpu.roll(tail, k, 0)
    row = lax.broadcasted_iota(jnp.int32, tail.shape, 0)
    first = jnp.where(row < k, rp, r[0:SUBLANES])
    return jnp.concatenate([first, r[SUBLANES:]], axis=0)


def _causal_conv_tile(cur, tail, w, b):
    width = w.shape[0]
    out = b + w[width - 1:width] * cur
    for k in range(1, width):
        out = out + w[width - 1 - k:width - k] * _shift_rows(cur, tail, k)
    return out


def _lru_gates(u, wg_ref, bg, lam):
    pair = 2 * LRU_BLOCK
    n_pair = D_RNN // pair
    ub = u.astype(bf16)
    gates = []
    for gi in range(2):
        cols = [jnp.dot(ub[:, p * pair:(p + 1) * pair], wg_ref[gi, p], preferred_element_type=f32) for p in range(n_pair)]
        gates.append(jnp.concatenate(cols, axis=1) + bg[gi:gi + 1])
    r = jax.nn.sigmoid(gates[0])
    i = jax.nn.sigmoid(gates[1])
    z = -lam
    softplus = jnp.maximum(z, 0.0) + jnp.log1p(jnp.exp(-jnp.abs(z)))
    log_a = -LRU_C * r * softplus
    a = jnp.exp(log_a)
    mult = jnp.sqrt(-jnp.tanh(log_a) * (a * a + 1.0))
    return a, mult * (i * u)


def _scan_rows(a, b):
    n = a.shape[0]
    row = lax.broadcasted_iota(jnp.int32, a.shape, 0)
    s = 1
    while s < n:
        a_sh = jnp.where(row >= s, pltpu.roll(a, s, 0), 1.0)
        b_sh = jnp.where(row >= s, pltpu.roll(b, s, 0), 0.0)
        b = a * b_sh + b
        a = a * a_sh
        s *= 2
    return a, b


def _rope_cols(x, cos_t, sin_t):
    lane = lax.broadcasted_iota(jnp.int32, x.shape, 1)
    half = ROT_DIM // 2
    swapped = jnp.where(lane % HEAD_DIM < half, pltpu.roll(x, LANES - half, 1), pltpu.roll(x, half, 1))
    return x * cos_t + swapped * sin_t


def _rope_wide(x, cos_t, sin_t):
    n = x.shape[1] // LANES
    return jnp.concatenate([_rope_cols(x[:, j * LANES:(j + 1) * LANES], cos_t, sin_t) for j in range(n)], axis=1)


def _topk_mask(imp, cur, n_blocks):
    blk = lax.broadcasted_iota(jnp.int32, imp.shape, 1)
    forced = (blk == 0) | (blk == cur) | (blk == cur - 1)
    imp = jnp.where(forced, FORCED_SCORE, imp)
    imp = jnp.where(blk <= cur, imp, NEG_INF)
    v = jnp.where(blk < n_blocks, imp, BELOW_ALL)
    rank = jnp.zeros(imp.shape, jnp.int32)
    for i in range(n_blocks):
        vi = v[:, i:i + 1]
        rank = rank + ((vi > v) | ((vi == v) & (blk > i))).astype(jnp.int32)
    return jnp.where((rank < TOP_N) & (v > 0.5 * NEG_INF), 1.0, 0.0)


def _mod_body(c_ref, w_ref, b_ref, o_ref):
    c = c_ref[...]
    o_ref[0] = _mm(c * jax.nn.sigmoid(c), w_ref[0].astype(bf16)) + b_ref[0]


def _ada_mod(c, w, b):
    n_l, d, n = w.shape
    m = c.shape[0]
    tn = 1024 if n % 1024 == 0 else n
    return pl.pallas_call(
        _mod_body,
        grid=(n_l, n // tn),
        in_specs=[pl.BlockSpec((m, d), lambda l, j: (0, 0)),
                  pl.BlockSpec((1, d, tn), lambda l, j: (l, 0, j)),
                  pl.BlockSpec((1, 1, tn), lambda l, j: (l, 0, j))],
        out_specs=pl.BlockSpec((1, m, tn), lambda l, j: (l, 0, j)),
        out_shape=jax.ShapeDtypeStruct((n_l, m, n), f32),
        compiler_params=_cparams(("arbitrary", "arbitrary")),
        name="ada_mod",
    )(c, w, b.reshape(n_l, 1, n))


def _lru_prompt_body(x_ref, mod_ref, g_ref, win_ref, bin_ref, wc_ref, bc_ref, wg_ref, bg_ref, lam_ref, wout_ref,
                     xo_ref, hlast_ref, cbuf_ref, tail_ref, h_ref):
    t = pl.program_id(1)

    @pl.when(t == 0)
    def _():
        tail_ref[...] = jnp.zeros(tail_ref.shape, f32)
        h_ref[...] = jnp.zeros(h_ref.shape, f32)

    x = x_ref[0]
    tm = x.shape[0]
    mod = mod_ref[0]
    g = g_ref[...]
    h = _rms(x, g[0:1]) * (1.0 + mod[1:2]) + mod[0:1]
    proj = _mm(h, win_ref[...]) + bin_ref[...]
    y = _gelu(proj[:, :D_RNN])
    up = proj[:, D_RNN:]
    u = _causal_conv_tile(up, tail_ref[...], wc_ref[...], bc_ref[...])
    last_rows = up[tm - SUBLANES:tm]
    tail_ref[...] = last_rows
    cbuf_ref[0] = last_rows
    a, bx = _lru_gates(u, wg_ref, bg_ref[...], lam_ref[...])
    a_cum, hs = _scan_rows(a, bx)
    hs = hs + a_cum * h_ref[0:1]
    h_fin = hs[tm - 1:tm]
    h_ref[...] = jnp.broadcast_to(h_fin, h_ref.shape)
    hlast_ref[0] = jnp.broadcast_to(h_fin, hlast_ref.shape[1:])
    out = _mm(hs * y, wout_ref[...])
    xo_ref[0] = x + mod[2:3] * _rms(out, g[1:2])


def _lru_prompt(x, mod, g, w_in, b_in, w_conv, b_conv, wg, bg, lam, w_out):
    bsz, t, d = x.shape
    tm = min(ROW_TILE, t)
    row_spec = pl.BlockSpec((1, tm, d), lambda b, i: (b, i, 0))
    small = lambda r, c: pl.BlockSpec((1, r, c), lambda b, i: (b, 0, 0))
    return pl.pallas_call(
        _lru_prompt_body,
        grid=(bsz, t // tm),
        in_specs=[row_spec, small(N_MOD, d), _const_spec(g.shape), _const_spec(w_in.shape), _const_spec(b_in.shape),
                  _const_spec(w_conv.shape), _const_spec(b_conv.shape), _const_spec(wg.shape), _const_spec(bg.shape),
                  _const_spec(lam.shape), _const_spec(w_out.shape)],
        out_specs=[row_spec, small(SUBLANES, D_RNN), small(SUBLANES, D_RNN)],
        out_shape=[jax.ShapeDtypeStruct((bsz, t, d), f32), jax.ShapeDtypeStruct((bsz, SUBLANES, D_RNN), f32),
                   jax.ShapeDtypeStruct((bsz, SUBLANES, D_RNN), f32)],
        scratch_shapes=[pltpu.VMEM((SUBLANES, D_RNN), f32), pltpu.VMEM((SUBLANES, D_RNN), f32)],
        compiler_params=_cparams(("arbitrary", "arbitrary")),
        name="lru_prompt",
    )(x, mod, g, w_in, b_in, w_conv, b_conv, wg, bg, lam, w_out)


def _lru_sample_body(x_ref, mod_ref, g_ref, h0_ref, taps_ref, win_ref, bin_ref, wc_ref, bc_ref, wg_ref, bg_ref, lam_ref,
                     wout_ref, xo_ref, hnew_ref, up_ref):
    x = x_ref[...]
    g = g_ref[...]
    h = _rms(x, g[0:1]) * (1.0 + mod_ref[1]) + mod_ref[0]
    proj = _mm(h, win_ref[...]) + bin_ref[...]
    y = _gelu(proj[:, :D_RNN])
    up = proj[:, D_RNN:]
    up_ref[...] = up
    wc = wc_ref[...]
    u = bc_ref[...] + wc[LRU_CONV_W - 1:LRU_CONV_W] * up
    for k in range(LRU_CONV_W - 1):
        u = u + wc[k:k + 1] * taps_ref[k]
    a, bx = _lru_gates(u, wg_ref, bg_ref[...], lam_ref[...])
    hs = a * h0_ref[...] + bx
    hnew_ref[...] = hs
    out = _mm(hs * y, wout_ref[...])
    xo_ref[...] = x + mod_ref[2] * _rms(out, g[1:2])


def _lru_sample(x, mod, g, h0, taps, w_in, b_in, w_conv, b_conv, wg, bg, lam, w_out):
    m, d = x.shape
    return pl.pallas_call(
        _lru_sample_body,
        out_shape=[jax.ShapeDtypeStruct((m, d), f32), jax.ShapeDtypeStruct((m, D_RNN), f32),
                   jax.ShapeDtypeStruct((m, D_RNN), f32)],
        compiler_params=_cparams(),
        name="lru_sample",
    )(x, mod, g, h0, taps, w_in, b_in, w_conv, b_conv, wg, bg, lam, w_out)


def _ffn_core(h, tap_fn, wi_ref, wc, bc, wo_ref, store_fn):
    acc = jnp.zeros((h.shape[0], D_MODEL), f32)
    hb = h.astype(bf16)
    for c in range(D_FF // FF_CHUNK):
        zs = []
        for base in (0, D_FF):
            lo = base + c * FF_CHUNK
            z = jnp.dot(hb, wi_ref[:, lo:lo + FF_CHUNK], preferred_element_type=f32)
            store_fn(z, lo)
            zc = bc[:, lo:lo + FF_CHUNK] + wc[FFN_CONV_W - 1:FFN_CONV_W, lo:lo + FF_CHUNK] * z
            for k in range(1, FFN_CONV_W):
                zc = zc + wc[FFN_CONV_W - 1 - k:FFN_CONV_W - k, lo:lo + FF_CHUNK] * tap_fn(z, lo, k)
            zs.append(zc)
        gz = _gelu(zs[0]) * zs[1]
        acc = acc + jnp.dot(gz.astype(bf16), wo_ref[c * FF_CHUNK:(c + 1) * FF_CHUNK, :], preferred_element_type=f32)
    return acc


def _ffn_prompt_body(has_pre, *refs):
    if has_pre:
        (x_ref, o_ref, wpre_ref, mod_ref, g_ref, wi_ref, wc_ref, bc_ref, wo_ref, xo_ref, fbuf_ref, tail_ref) = refs
    else:
        (x_ref, mod_ref, g_ref, wi_ref, wc_ref, bc_ref, wo_ref, xo_ref, fbuf_ref, tail_ref) = refs
    t = pl.program_id(1)

    @pl.when(t == 0)
    def _():
        tail_ref[...] = jnp.zeros(tail_ref.shape, f32)

    x = x_ref[0]
    tm = x.shape[0]
    mod = mod_ref[0]
    g = g_ref[...]
    if has_pre:
        x = x + mod[2:3] * _rms(jnp.dot(o_ref[0], wpre_ref[...], preferred_element_type=f32), g[1:2])
    h = _rms(x, g[2:3]) * (1.0 + mod[4:5]) + mod[3:4]

    def tap(z, lo, k):
        return _shift_rows(z, tail_ref[:, lo:lo + FF_CHUNK], k)

    def store(z, lo):
        fbuf_ref[0, :, lo:lo + FF_CHUNK] = z[tm - SUBLANES:tm]

    acc = _ffn_core(h, tap, wi_ref, wc_ref[...], bc_ref[...], wo_ref, store)
    tail_ref[...] = fbuf_ref[0]
    xo_ref[0] = x + mod[5:6] * _rms(acc, g[3:4])


def _ffn_prompt(x, mod, g, w_in, w_conv, b_conv, w_out, pre=None):
    bsz, t, d = x.shape
    tm = min(FFN_TILE, t)
    row_spec = pl.BlockSpec((1, tm, d), lambda b, i: (b, i, 0))
    small = lambda r, c: pl.BlockSpec((1, r, c), lambda b, i: (b, 0, 0))
    ins, specs = [x], [row_spec]
    if pre is not None:
        o_att, w_pre = pre
        ins += [o_att, w_pre]
        specs += [pl.BlockSpec((1, tm, o_att.shape[2]), lambda b, i: (b, i, 0)), _const_spec(w_pre.shape)]
    ins += [mod, g, w_in, w_conv, b_conv, w_out]
    specs += [small(N_MOD, d), _const_spec(g.shape), _const_spec(w_in.shape), _const_spec(w_conv.shape),
              _const_spec(b_conv.shape), _const_spec(w_out.shape)]
    return pl.pallas_call(
        functools.partial(_ffn_prompt_body, pre is not None),
        grid=(bsz, t // tm),
        in_specs=specs,
        out_specs=[row_spec, small(SUBLANES, 2 * D_FF)],
        out_shape=[jax.ShapeDtypeStruct((bsz, t, d), f32), jax.ShapeDtypeStruct((bsz, SUBLANES, 2 * D_FF), f32)],
        scratch_shapes=[pltpu.VMEM((SUBLANES, 2 * D_FF), f32)],
        compiler_params=_cparams(("arbitrary", "arbitrary")),
        name="ffn_prompt",
    )(*ins)


def _ffn_sample_body(has_pre, *refs):
    if has_pre:
        (x_ref, o_ref, wpre_ref, mod_ref, g_ref, taps_ref, wi_ref, wc_ref, bc_ref, wo_ref, xo_ref, z_ref) = refs
    else:
        (x_ref, mod_ref, g_ref, taps_ref, wi_ref, wc_ref, bc_ref, wo_ref, xo_ref, z_ref) = refs
    x = x_ref[...]
    g = g_ref[...]
    if has_pre:
        x = x + mod_ref[2] * _rms(jnp.dot(o_ref[...], wpre_ref[...], preferred_element_type=f32), g[1:2])
    h = _rms(x, g[2:3]) * (1.0 + mod_ref[4]) + mod_ref[3]

    def tap(z, lo, k):
        return taps_ref[FFN_CONV_W - 1 - k, :, lo:lo + FF_CHUNK]

    def store(z, lo):
        z_ref[:, lo:lo + FF_CHUNK] = z

    acc = _ffn_core(h, tap, wi_ref, wc_ref[...], bc_ref[...], wo_ref, store)
    xo_ref[...] = x + mod_ref[5] * _rms(acc, g[3:4])


def _ffn_sample(x, mod, g, taps, w_in, w_conv, b_conv, w_out, pre=None):
    m, d = x.shape
    ins = [x] + (list(pre) if pre is not None else []) + [mod, g, taps, w_in, w_conv, b_conv, w_out]
    return pl.pallas_call(
        functools.partial(_ffn_sample_body, pre is not None),
        out_shape=[jax.ShapeDtypeStruct((m, d), f32), jax.ShapeDtypeStruct((m, 2 * D_FF), f32)],
        compiler_params=_cparams(),
        name="ffn_sample",
    )(*ins)


def _head_cols(src, extra):
    lane = lax.broadcasted_iota(jnp.int32, (src.shape[0], LANES), 1)
    out = []
    for j in range(KV_WIDTH // LANES):
        col = src[:, j * LANES:(j + 1) * LANES]
        out.append(jnp.where(lane < HEAD_DIM, col, extra))
        out.append(jnp.where(lane < HEAD_DIM, pltpu.roll(col, HEAD_DIM, 1), extra))
    return out


def _kv_prompt_body(x_ref, mod_ref, g_ref, w_ref, wvt_ref, cos_ref, sin_ref, rows_ref, win_ref, kaug_ref, kwin_ref, vst_ref, vwt_ref):
    x = x_ref[0]
    tm = x.shape[0]
    mod = mod_ref[0]
    h = _rms(x, g_ref[...]) * (1.0 + mod[1:2]) + mod[0:1]
    hb = h.astype(bf16)
    kv = jnp.dot(hb, w_ref[...], preferred_element_type=f32)
    vt = _mm_nt(wvt_ref[...], hb)
    cos_t, sin_t = cos_ref[...], sin_ref[...]
    w = KV_WIDTH
    k_sel = _rope_wide(kv[:, 2 * w:3 * w], cos_t, sin_t)
    k_win = _rope_wide(kv[:, 4 * w:5 * w], cos_t, sin_t)
    rows_ref[0, :, 0:2 * w] = kv[:, 0:2 * w]
    rows_ref[0, :, 2 * w:3 * w] = k_sel
    rows_ref[0, :, 3 * w:4 * w] = kv[:, 3 * w:4 * w]
    win_ref[0, :, 0:w] = k_win
    win_ref[0, :, w:2 * w] = kv[:, 5 * w:6 * w]
    row = lax.broadcasted_iota(jnp.int32, (tm, LANES), 0)
    lane = lax.broadcasted_iota(jnp.int32, (tm, LANES), 1)
    pos = pl.program_id(1) * tm + row
    onehot = jnp.where(lane - HEAD_DIM == (pos // L_SEL) % SEL_GROUP, 1.0, 0.0)
    for gi, col in enumerate(_head_cols(k_sel, onehot)):
        for c in range(tm // K_CHUNK):
            kaug_ref[0, gi, c] = col[c * K_CHUNK:(c + 1) * K_CHUNK].astype(bf16)
    for gi, col in enumerate(_head_cols(k_win, 0.0)):
        for c in range(tm // Q_TILE):
            kwin_ref[0, gi, c] = col[c * Q_TILE:(c + 1) * Q_TILE, 0:HEAD_DIM].astype(bf16)
    for gi in range(N_KV_HEADS):
        v_s = vt[gi * HEAD_DIM:(gi + 1) * HEAD_DIM].astype(bf16)
        v_w = vt[w + gi * HEAD_DIM:w + (gi + 1) * HEAD_DIM].astype(bf16)
        for c in range(tm // K_CHUNK):
            vst_ref[0, gi, c] = v_s[:, c * K_CHUNK:(c + 1) * K_CHUNK]
        for c in range(tm // Q_TILE):
            vwt_ref[0, gi, c] = v_w[:, c * Q_TILE:(c + 1) * Q_TILE]


def _kv_prompt(x, mod, g_kv, w_kv, w_vt, cos_t, sin_t):
    bsz, t, d = x.shape
    tm = K_CHUNK
    nc, nt = tm // K_CHUNK, tm // Q_TILE
    row = lambda c: pl.BlockSpec((1, tm, c), lambda b, i: (b, i, 0))
    chunked = lambda n, r, c: pl.BlockSpec((1, N_KV_HEADS, n, r, c), lambda b, i: (b, 0, i, 0, 0))
    shape5 = lambda n, r, c: jax.ShapeDtypeStruct((bsz, N_KV_HEADS, n, r, c), bf16)
    return pl.pallas_call(
        _kv_prompt_body,
        grid=(bsz, t // tm),
        in_specs=[row(d), pl.BlockSpec((1, 2, d), lambda b, i: (b, 0, 0)), _const_spec(g_kv.shape), _const_spec(w_kv.shape),
                  _const_spec(w_vt.shape), pl.BlockSpec((tm, LANES), lambda b, i: (i, 0)),
                  pl.BlockSpec((tm, LANES), lambda b, i: (i, 0))],
        out_specs=[row(4 * KV_WIDTH), row(2 * KV_WIDTH), chunked(nc, K_CHUNK, LANES), chunked(nt, Q_TILE, HEAD_DIM),
                   chunked(nc, HEAD_DIM, K_CHUNK), chunked(nt, HEAD_DIM, Q_TILE)],
        out_shape=[jax.ShapeDtypeStruct((bsz, t, 4 * KV_WIDTH), f32), jax.ShapeDtypeStruct((bsz, t, 2 * KV_WIDTH), f32),
                   shape5(t // K_CHUNK, K_CHUNK, LANES), shape5(t // Q_TILE, Q_TILE, HEAD_DIM),
                   shape5(t // K_CHUNK, HEAD_DIM, K_CHUNK), shape5(t // Q_TILE, HEAD_DIM, Q_TILE)],
        compiler_params=_cparams(("arbitrary", "arbitrary")),
        name="kv_prompt",
    )(x, mod, g_kv, w_kv, w_vt, cos_t, sin_t)


def _kv_sample_body(x_ref, mod_ref, g_ref, w_ref, cos_ref, sin_ref, rows_ref, win_ref):
    h = _rms(x_ref[...], g_ref[...]) * (1.0 + mod_ref[1]) + mod_ref[0]
    kv = _mm(h, w_ref[...])
    cos_t, sin_t = cos_ref[...], sin_ref[...]
    w = KV_WIDTH
    rows_ref[:, 0:2 * w] = kv[:, 0:2 * w]
    rows_ref[:, 2 * w:3 * w] = _rope_wide(kv[:, 2 * w:3 * w], cos_t, sin_t)
    rows_ref[:, 3 * w:4 * w] = kv[:, 3 * w:4 * w]
    win_ref[:, 0:w] = _rope_wide(kv[:, 4 * w:5 * w], cos_t, sin_t)
    win_ref[:, w:2 * w] = kv[:, 5 * w:6 * w]


def _kv_sample(x, mod, g_kv, w_kv, cos_t, sin_t):
    m = x.shape[0]
    return pl.pallas_call(
        _kv_sample_body,
        out_shape=[jax.ShapeDtypeStruct((m, 4 * KV_WIDTH), f32), jax.ShapeDtypeStruct((m, 2 * KV_WIDTH), f32)],
        compiler_params=_cparams(),
        name="kv_sample",
    )(x, mod, g_kv, w_kv, cos_t, sin_t)


def _compress_body(per_group, *refs):
    n_slab = 2 * KV_WIDTH // LANES
    slabs = refs[:n_slab]
    pe_ref, w1_ref, wp_ref, b1_ref, w2_ref = refs[n_slab:n_slab + 5]
    outs = refs[n_slab + 5:]
    n_chunk = slabs[0].shape[1] // D_CMP
    per_slot = KV_WIDTH // LANES
    hid = CMP_HIDDEN
    for s in range(2):
        const = _mm(jnp.broadcast_to(pe_ref[s], (SUBLANES, L_CMP * HEAD_DIM)), w1_ref[s])[0:1] + b1_ref[s]
        tok = jnp.zeros((n_chunk, KV_WIDTH), f32)
        for j in range(per_slot):
            slab = slabs[s * per_slot + j]
            acc = jnp.zeros((n_chunk, 4 * hid), f32)
            for rp in range(D_CMP // 2):
                lhs = jnp.concatenate([slab[0, pl.ds(2 * rp, n_chunk, stride=D_CMP), :],
                                       slab[0, pl.ds(2 * rp + 1, n_chunk, stride=D_CMP), :]], axis=1)
                acc = acc + _mm(lhs, wp_ref[s, rp])
            for hl in range(2):
                gi = 2 * j + hl
                first = acc[:, 2 * hl * hid:(2 * hl + 1) * hid]
                second = acc[:, (2 * hl + 1) * hid:(2 * hl + 2) * hid]
                act_g = _gelu(first + pltpu.roll(second, n_chunk - 1, 0) + const).astype(bf16)
                if not per_group:
                    tok = tok + jnp.dot(act_g, w2_ref[s, gi], preferred_element_type=f32)
                elif s == 0:
                    outs[s][0, gi] = jnp.dot(act_g, w2_ref[s], preferred_element_type=f32)[:, 0:HEAD_DIM].astype(bf16)
                else:
                    outs[s][0, gi] = _mm_nt(w2_ref[s], act_g)[0:HEAD_DIM].astype(bf16)
        if not per_group:
            outs[s][0] = tok.astype(bf16)


def _compress(rows, cw, per_group):
    bsz, t = rows.shape[:2]
    n_chunk = t // D_CMP
    n_slab = 2 * KV_WIDTH // LANES
    slab_specs = [pl.BlockSpec((1, t, LANES), functools.partial(lambda j, b: (b, 0, j), j)) for j in range(n_slab)]
    pe, wt, wb, b1, w2 = cw
    if per_group:
        dims = [(n_chunk, HEAD_DIM), (HEAD_DIM, n_chunk)]
        out_specs = [pl.BlockSpec((1, N_KV_HEADS) + dm, lambda b: (b, 0, 0, 0)) for dm in dims]
        out_shapes = [jax.ShapeDtypeStruct((bsz, N_KV_HEADS) + dm, bf16) for dm in dims]
    else:
        out_specs = [pl.BlockSpec((1, n_chunk, KV_WIDTH), lambda b: (b, 0, 0))] * 2
        out_shapes = [jax.ShapeDtypeStruct((bsz, n_chunk, KV_WIDTH), bf16)] * 2
    return pl.pallas_call(
        functools.partial(_compress_body, per_group),
        grid=(bsz,),
        in_specs=slab_specs + [_const_spec(pe.shape), _const_spec(wt.shape), _const_spec(wb.shape), _const_spec(b1.shape),
                               _const_spec(w2.shape)],
        out_specs=out_specs,
        out_shape=out_shapes,
        compiler_params=_cparams(("arbitrary",)),
        name="compress_prompt" if per_group else "compress_sample",
    )(*([rows] * n_slab), pe, wt, wb, b1, w2)


def _q_prompt_body(x_ref, mod_ref, g_ref, wqt_ref, wgt_ref, cos_ref, sin_ref, qraw_ref, qrot_ref, gate_ref):
    mod = mod_ref[0]
    h = _rms(x_ref[0], g_ref[0:1]) * (1.0 + mod[1:2]) + mod[0:1]
    hb = h.astype(bf16)
    qt = _mm_nt(wqt_ref[...], hb)
    gate_ref[0] = jax.nn.sigmoid(_mm_nt(wgt_ref[...], hb))
    scale = HEAD_DIM ** -0.5
    qraw_ref[0] = (qt * scale).astype(bf16)
    half = ROT_DIM // 2
    dim = lax.broadcasted_iota(jnp.int32, qt.shape, 0) % HEAD_DIM
    swapped = jnp.where(dim < half, pltpu.roll(qt, Q_WIDTH - half, 0), pltpu.roll(qt, half, 0))
    cos_t = jnp.concatenate([cos_ref[...]] * N_HEADS, axis=0)
    sin_t = jnp.concatenate([sin_ref[...]] * N_HEADS, axis=0)
    qrot_ref[0] = ((qt * cos_t + swapped * sin_t) * scale).astype(bf16)


def _q_prompt(x, mod, g, wqt, wgt, cos_tt, sin_tt):
    bsz, t, d = x.shape
    tm = min(ROW_TILE, t)
    col = lambda r: pl.BlockSpec((1, r, tm), lambda b, i: (b, 0, i))
    tab = pl.BlockSpec((HEAD_DIM, tm), lambda b, i: (0, i))
    return pl.pallas_call(
        _q_prompt_body,
        grid=(bsz, t // tm),
        in_specs=[pl.BlockSpec((1, tm, d), lambda b, i: (b, i, 0)), pl.BlockSpec((1, N_MOD, d), lambda b, i: (b, 0, 0)),
                  _const_spec(g.shape), _const_spec(wqt.shape), _const_spec(wgt.shape), tab, tab],
        out_specs=[col(Q_WIDTH), col(Q_WIDTH), col(wgt.shape[0])],
        out_shape=[jax.ShapeDtypeStruct((bsz, Q_WIDTH, t), bf16), jax.ShapeDtypeStruct((bsz, Q_WIDTH, t), bf16),
                   jax.ShapeDtypeStruct((bsz, wgt.shape[0], t), f32)],
        compiler_params=_cparams(("arbitrary", "arbitrary")),
        name="q_prompt",
    )(x, mod, g, wqt, wgt, cos_tt, sin_tt)


def _q_sample_body(x_ref, mod_ref, g_ref, wq_ref, wgt_ref, cos_ref, sin_ref, qraw_ref, qrot_ref, gate_ref):
    h = _rms(x_ref[...], g_ref[0:1]) * (1.0 + mod_ref[1]) + mod_ref[0]
    hb = h.astype(bf16)
    q = jnp.dot(hb, wq_ref[...], preferred_element_type=f32)
    gate_ref[...] = jax.nn.sigmoid(jnp.dot(hb, wgt_ref[...], preferred_element_type=f32))
    scale = HEAD_DIM ** -0.5
    qraw_ref[...] = (q * scale).astype(bf16)
    qrot_ref[...] = (_rope_wide(q, cos_ref[...], sin_ref[...]) * scale).astype(bf16)


def _q_sample(x, mod, g, wq, wgt, cos_t, sin_t):
    m = x.shape[0]
    return pl.pallas_call(
        _q_sample_body,
        out_shape=[jax.ShapeDtypeStruct((m, wq.shape[1]), bf16), jax.ShapeDtypeStruct((m, wq.shape[1]), bf16),
                   jax.ShapeDtypeStruct((m, wgt.shape[1]), f32)],
        compiler_params=_cparams(),
        name="q_sample",
    )(x, mod, g, wq, wgt, cos_t, sin_t)


def _rank_select(imp, cur, n_blocks):
    blk = lax.broadcasted_iota(jnp.int32, imp.shape, 0)
    forced = (blk == 0) | (blk == cur) | (blk == cur - 1)
    v = jnp.where(forced, FORCED_SCORE, imp)
    v = jnp.where(blk <= cur, v, NEG_INF)
    v = jnp.where(blk < n_blocks, v, BELOW_ALL)
    rank = jnp.zeros(imp.shape, jnp.int32)
    for i in range(n_blocks):
        vi = v[i:i + 1, :]
        rank = rank + ((vi > v) | ((vi == v) & (blk > i))).astype(jnp.int32)
    return (rank < TOP_N) & (v > 0.5 * NEG_INF)


def _attn_prompt_body(n_cmp, n_sel, qraw_ref, qrot_ref, gate_ref, kc_ref, vct_ref, c2st_ref, kaug_ref, vst_ref, kwin_ref,
                      vwt_ref, o_ref, qaug_ref, negsel_ref):
    qi = pl.program_id(2)
    tq = qraw_ref.shape[2]
    n4 = GROUP * tq
    q_chunk = (qi * tq) // K_CHUNK

    def stack(ref):
        return jnp.concatenate([ref[0, r * HEAD_DIM:(r + 1) * HEAD_DIM, :] for r in range(GROUP)], axis=1)

    q_raw = stack(qraw_ref)
    q_rot = stack(qrot_ref)
    q_pos = qi * tq + lax.broadcasted_iota(jnp.int32, (1, n4), 1) % tq

    s = jnp.dot(kc_ref[0, 0], q_raw, preferred_element_type=f32)
    blk = lax.broadcasted_iota(jnp.int32, s.shape, 0)
    mask = (blk * D_CMP + (L_CMP - 1) <= q_pos) & (blk < n_cmp)
    s = jnp.where(mask, s, NEG_INF)
    p = jnp.where(mask, jnp.exp(s - jnp.max(s, axis=0, keepdims=True)), 0.0)
    l = jnp.sum(p, axis=0, keepdims=True)
    p = p / jnp.where(l > 0.0, l, 1.0)
    o_c = jnp.dot(vct_ref[0, 0], p.astype(bf16), preferred_element_type=f32)

    p_group = p[:, 0:tq]
    for r in range(1, GROUP):
        p_group = p_group + p[:, r * tq:(r + 1) * tq]
    imp = _mm_exact_lhs(c2st_ref[...], p_group)
    n_rows = negsel_ref.shape[0]
    picked = _rank_select(imp[0:n_rows], q_pos[:, 0:tq] // L_SEL, n_sel)
    negsel = jnp.where(picked, 0.0, NEG_INF)
    negsel_ref[...] = jnp.concatenate([negsel] * GROUP, axis=1)

    qaug_ref[0:HEAD_DIM, :] = q_rot
    qaug_ref[HEAD_DIM + SEL_GROUP:, :] = jnp.zeros((LANES - HEAD_DIM - SEL_GROUP, n4), bf16)
    blocks_per_chunk = K_CHUNK // L_SEL

    def chunk_scores(c):
        first = pl.multiple_of((c * blocks_per_chunk // SEL_GROUP) * SEL_GROUP, SEL_GROUP)
        qaug_ref[HEAD_DIM:HEAD_DIM + SEL_GROUP, :] = negsel_ref[pl.ds(first, SEL_GROUP), :].astype(bf16)
        return jnp.dot(kaug_ref[0, 0, c], qaug_ref[...], preferred_element_type=f32)

    def update(sc, carry, vt):
        m_old, l_old, acc_old = carry
        m_new = jnp.maximum(m_old, jnp.max(sc, axis=0, keepdims=True))
        alpha = jnp.exp(m_old - m_new)
        pc = jnp.exp(sc - m_new)
        l_new = alpha * l_old + jnp.sum(pc, axis=0, keepdims=True)
        acc_new = alpha * acc_old + jnp.dot(vt, pc.astype(bf16), preferred_element_type=f32)
        return m_new, l_new, acc_new

    init = (jnp.full((1, n4), NEG_INF, f32), jnp.zeros((1, n4), f32), jnp.zeros((HEAD_DIM, n4), f32))
    carry = lax.fori_loop(0, q_chunk, lambda c, cr: update(chunk_scores(c), cr, vst_ref[0, 0, c]), init)
    sc = chunk_scores(q_chunk)
    key_pos = q_chunk * K_CHUNK + lax.broadcasted_iota(jnp.int32, sc.shape, 0)
    sc = jnp.where(key_pos <= q_pos, sc, NEG_INF)
    _, l_s, acc_s = update(sc, carry, vst_ref[0, 0, q_chunk])

    n_back = WINDOW // tq
    row = lax.broadcasted_iota(jnp.int32, (tq, n4), 0)
    col = lax.broadcasted_iota(jnp.int32, (tq, n4), 1) % tq
    tiles, vts = [], []
    for j in range(n_back + 1):
        ti = qi - n_back + j
        tic = jnp.maximum(ti, 0)
        sw = jnp.dot(kwin_ref[0, 0, tic], q_rot, preferred_element_type=f32)
        if j < n_back:
            sw = sw + jnp.where(ti >= 0, 0.0, NEG_INF)
        if j == 0:
            sw = jnp.where(row > col, sw, NEG_INF)
        elif j == n_back:
            sw = jnp.where(row <= col, sw, NEG_INF)
        tiles.append(sw)
        vts.append(vwt_ref[0, 0, tic])
    sw = jnp.concatenate(tiles, axis=0)
    pw = jnp.exp(sw - jnp.max(sw, axis=0, keepdims=True))
    l_w = jnp.sum(pw, axis=0, keepdims=True)
    acc_w = jnp.dot(jnp.concatenate(vts, axis=1), pw.astype(bf16), preferred_element_type=f32)

    gate = gate_ref[0]
    o_s = acc_s / l_s
    o_w = acc_w / l_w
    heads = []
    for r in range(GROUP):
        sl = slice(r * tq, (r + 1) * tq)
        heads.append(gate[3 * r:3 * r + 1] * o_c[:, sl] + gate[3 * r + 1:3 * r + 2] * o_s[:, sl]
                     + gate[3 * r + 2:3 * r + 3] * o_w[:, sl])
    o_ref[0] = jnp.concatenate(heads, axis=0).T.astype(o_ref.dtype)


def _attn_prompt(qt_raw, qt_rot, gates_t, kc, vct, c2st, kaug, vst, kwin, vwt, n_cmp, n_sel):
    bsz, _, t = qt_raw.shape
    tq = Q_TILE
    gw = GROUP * HEAD_DIM
    n4 = GROUP * tq
    q_spec = pl.BlockSpec((1, gw, tq), lambda b, g, i: (b, g, i))
    gate_rows = gates_t.shape[1] // N_KV_HEADS
    per_group = lambda a: pl.BlockSpec((1, 1) + a.shape[2:], lambda b, g, i: (b, g) + (0,) * (a.ndim - 2))
    sel_rows = -(-n_sel // SEL_GROUP) * SEL_GROUP
    return pl.pallas_call(
        functools.partial(_attn_prompt_body, n_cmp, n_sel),
        grid=(bsz, N_KV_HEADS, t // tq),
        in_specs=[q_spec, q_spec, pl.BlockSpec((1, gate_rows, tq), lambda b, g, i: (b, g, i)), per_group(kc), per_group(vct),
                  pl.BlockSpec(c2st.shape, lambda b, g, i: (0, 0)), per_group(kaug), per_group(vst), per_group(kwin),
                  per_group(vwt)],
        out_specs=pl.BlockSpec((1, tq, gw), lambda b, g, i: (b, i, g)),
        out_shape=jax.ShapeDtypeStruct((bsz, t, Q_WIDTH), bf16),
        scratch_shapes=[pltpu.VMEM((LANES, n4), bf16), pltpu.VMEM((sel_rows, n4), f32)],
        compiler_params=_cparams(("arbitrary", "arbitrary", "arbitrary")),
        name="attn_prompt",
    )(qt_raw, qt_rot, gates_t, kc, vct, c2st, kaug, vst, kwin, vwt)


def _attn_sample_body(n_cmp, n_sel, win_buf, qraw_ref, qrot_ref, gate_ref, kc_ref, vc_ref, c2s_ref, past_ref, new_ref,
                      wstate_ref, wnew_ref, o_ref):
    w = KV_WIDTH
    q_raw = qraw_ref[0]
    q_rot = qrot_ref[0]
    n_head = q_raw.shape[0]
    head = lax.broadcasted_iota(jnp.int32, (n_head, n_head), 0)
    head2 = lax.broadcasted_iota(jnp.int32, (n_head, n_head), 1)
    same_group = (head // GROUP == head2 // GROUP).astype(bf16)

    s = _mm_nt(q_raw, kc_ref[0])
    blk = lax.broadcasted_iota(jnp.int32, s.shape, 1)
    mask = blk < n_cmp
    s = jnp.where(mask, s, NEG_INF)
    p = jnp.where(mask, jnp.exp(s - jnp.max(s, axis=1, keepdims=True)), 0.0)
    p = p / jnp.sum(p, axis=1, keepdims=True)
    o_c = jnp.dot(p.astype(bf16), vc_ref[0], preferred_element_type=f32)
    imp = _mm_exact_rhs(_mm_exact_lhs(same_group, p), c2s_ref[...])
    cur = jnp.full((n_head, 1), n_sel - 1, jnp.int32)
    sel = _topk_mask(imp, cur, n_sel)

    def attend(q, k_past, v_past, mask_past, k_new, v_new, new_ok):
        sp = jnp.where(mask_past, _mm_nt(q, k_past.astype(bf16)), NEG_INF)
        sn = jnp.sum(q.astype(f32) * k_new.astype(bf16).astype(f32), axis=1, keepdims=True)
        sn = jnp.where(new_ok, sn, NEG_INF)
        m = jnp.maximum(jnp.max(sp, axis=1, keepdims=True), sn)
        pp = jnp.where(mask_past, jnp.exp(sp - m), 0.0)
        pn = jnp.where(new_ok, jnp.exp(sn - m), 0.0)
        l = jnp.sum(pp, axis=1, keepdims=True) + pn
        acc = jnp.dot(pp.astype(bf16), v_past.astype(bf16), preferred_element_type=f32)
        acc = acc + pn.astype(bf16).astype(f32) * v_new.astype(bf16).astype(f32)
        return acc / l

    n_past = past_ref.shape[1]
    jj = lax.broadcasted_iota(jnp.int32, (LANES, n_past), 0)
    kk = lax.broadcasted_iota(jnp.int32, (LANES, n_past), 1)
    expand = (jj == kk // L_SEL).astype(bf16)
    picked = jnp.dot(sel.astype(bf16), expand, preferred_element_type=f32) > 0.5
    new_row = new_ref[0]
    o_s = attend(q_rot, past_ref[0, :, 0:w], past_ref[0, :, w:2 * w], picked, new_row[:, 2 * w:3 * w], new_row[:, 3 * w:4 * w],
                 sel[:, n_sel - 1:n_sel] > 0.5)

    idx = lax.broadcasted_iota(jnp.int32, (n_head, win_buf), 1)
    in_window = win_buf - idx < WINDOW
    wnew = wnew_ref[0]
    o_w = attend(q_rot, wstate_ref[0, :, 0:w], wstate_ref[0, :, w:2 * w], in_window, wnew[:, 0:w], wnew[:, w:2 * w],
                 jnp.full((n_head, 1), True))

    gate = gate_ref[0]
    o = gate[:, 0:1] * o_c + gate[:, 1:2] * o_s + gate[:, 2:3] * o_w
    hrow = lax.broadcasted_iota(jnp.int32, o.shape, 0)
    lane = lax.broadcasted_iota(jnp.int32, o.shape, 1)
    o_ref[0] = jnp.where(lane // HEAD_DIM == hrow // GROUP, o, 0.0).astype(o_ref.dtype)


def _attn_sample(q_raw, q_rot, gates, kc, vc, c2s, past, new_rows, wstate, wnew, n_cmp, n_sel):
    bsz = q_raw.shape[0]
    n_past = past.shape[1]
    win_buf = wstate.shape[1]
    per_seq = lambda a: pl.BlockSpec((1,) + a.shape[1:], lambda b: (b,) + (0,) * (a.ndim - 1))
    return pl.pallas_call(
        functools.partial(_attn_sample_body, n_cmp, n_sel, win_buf),
        grid=(bsz,),
        in_specs=[per_seq(q_raw), per_seq(q_rot), per_seq(gates), per_seq(kc), per_seq(vc),
                  pl.BlockSpec(c2s.shape, lambda b: (0, 0)),
                  pl.BlockSpec((1, n_past, 2 * KV_WIDTH), lambda b: (b, 0, 1)), per_seq(new_rows), per_seq(wstate), per_seq(wnew)],
        out_specs=pl.BlockSpec((1, N_HEADS, KV_WIDTH), lambda b: (b, 0, 0)),
        out_shape=jax.ShapeDtypeStruct((bsz, N_HEADS, KV_WIDTH), bf16),
        compiler_params=_cparams(("arbitrary",)),
        name="attn_sample",
    )(q_raw, q_rot, gates, kc, vc, c2s, past, new_rows, wstate, wnew)


def _rope_tables(pos):
    half = ROT_DIM // 2
    inv = np.power(ROPE_THETA, -np.arange(half, dtype=np.float64) * (2.0 / ROT_DIM))
    ang = np.asarray(pos, np.float64)[:, None] * inv[None, :]
    cos_h = np.ones((len(pos), HEAD_DIM), np.float32)
    sin_h = np.zeros((len(pos), HEAD_DIM), np.float32)
    cos_h[:, :half] = np.cos(ang)
    cos_h[:, half:ROT_DIM] = np.cos(ang)
    sin_h[:, :half] = -np.sin(ang)
    sin_h[:, half:ROT_DIM] = np.sin(ang)
    return (jnp.asarray(np.tile(cos_h, (1, 2))), jnp.asarray(np.tile(sin_h, (1, 2))),
            jnp.asarray(cos_h.T.copy()), jnp.asarray(sin_h.T.copy()))


def _cmp_to_sel(t, n_rows, transposed):
    nc = (t - L_CMP) // D_CMP + 1
    ns = -(-t // L_SEL)
    c_start = np.arange(nc) * D_CMP
    s_start = np.arange(ns) * L_SEL
    m = np.zeros((n_rows, LANES), np.float32)
    m[:nc, :ns] = (c_start[:, None] < s_start[None, :] + L_SEL) & (c_start[:, None] + L_CMP > s_start[None, :])
    return jnp.asarray(m.T.copy() if transposed else m, bf16), nc, ns


def _gate_blockdiag(w_gate):
    wg = w_gate.astype(bf16).reshape(2, LRU_HEADS // 2, 2, LRU_BLOCK, LRU_BLOCK)
    z = jnp.zeros_like(wg[:, :, 0])
    top = jnp.concatenate([wg[:, :, 0], z], axis=-1)
    bot = jnp.concatenate([z, wg[:, :, 1]], axis=-1)
    return jnp.concatenate([top, bot], axis=-2)


def _compress_weights(pe_cmp, w_cmp1, b_cmp1, w_cmp2, per_group):
    pe = pe_cmp.reshape(2, 1, L_CMP * HEAD_DIM)
    w1 = w_cmp1.astype(bf16)
    w1p = w1.reshape(2, 2, D_CMP // 2, 2, HEAD_DIM, CMP_HIDDEN)
    eye2 = jnp.eye(2, dtype=bf16)
    wp = jnp.einsum('shpldk,ab->spladbhk', w1p, eye2).reshape(2, D_CMP // 2, 2 * LANES, 4 * CMP_HIDDEN)
    b1 = b_cmp1.reshape(2, 1, CMP_HIDDEN)
    w2 = w_cmp2.astype(bf16)
    eye = jnp.eye(N_KV_HEADS, dtype=bf16)
    if per_group:
        pad = jnp.zeros((CMP_HIDDEN, CMP_HIDDEN - HEAD_DIM), bf16)
        w2x = jnp.stack([jnp.concatenate([w2[0], pad], axis=1), jnp.concatenate([w2[1].T, pad.T], axis=0)])
    else:
        w2x = jnp.einsum('skd,gq->sgkqd', w2, eye).reshape(2, N_KV_HEADS, CMP_HIDDEN, KV_WIDTH)
    return pe, w1, wp, b1, w2x


def _q_weights_prompt(w_qg):
    wqt = w_qg[:, :Q_WIDTH].T.astype(bf16)
    wg = w_qg[:, Q_WIDTH:].T.reshape(N_KV_HEADS, 3 * GROUP, D_MODEL)
    wgt = jnp.pad(wg, ((0, 0), (0, 2 * SUBLANES - 3 * GROUP), (0, 0))).reshape(N_KV_HEADS * 2 * SUBLANES, D_MODEL).astype(bf16)
    return wqt, wgt


def _q_weights_sample(w_qg):
    wq = w_qg[:, :Q_WIDTH].astype(bf16).reshape(D_MODEL, N_KV_HEADS, GROUP, HEAD_DIM)
    eye = jnp.eye(N_KV_HEADS, dtype=bf16)
    wq = jnp.einsum('kgrd,gq->kgrqd', wq, eye).reshape(D_MODEL, N_HEADS * KV_WIDTH)
    wg = w_qg[:, Q_WIDTH:].reshape(D_MODEL, N_HEADS, 3)
    wg = jnp.pad(wg, ((0, 0), (0, 0), (0, LANES - 3))).reshape(D_MODEL, N_HEADS * LANES).astype(bf16)
    return wq, wg


def _out_weights_sample(w_o):
    w = w_o.astype(bf16).reshape(N_KV_HEADS, GROUP, HEAD_DIM, D_MODEL)
    eye = jnp.eye(N_KV_HEADS, dtype=bf16)
    return jnp.einsum('grdn,gq->grqdn', w, eye).reshape(N_HEADS * KV_WIDTH, D_MODEL)


def kernel(x_prompt, x_sample, c_prompt, c_sample, cache_nsa_kv, page_table, state_nsa_win, state_lru_h, state_lru_conv, state_ffn_conv, w_mod, b_mod, g_norm, w_lru_in, b_lru_in, w_lru_conv, b_lru_conv, w_lru_gate, b_lru_gate, lru_lambda, w_lru_out, g_kv, w_mod_kv, b_mod_kv, w_kv, pe_cmp, w_cmp1, b_cmp1, w_cmp2, w_nsa_qg, w_nsa_out, w_ffn_in, w_ffn_conv, b_ffn_conv, w_ffn_out):
    bp, tp, d = x_prompt.shape
    bs, ts, _ = x_sample.shape
    assert ts == 1 and d == D_MODEL
    depth = w_mod.shape[0]
    n_a = w_lru_in.shape[0]
    past_len = page_table.shape[1] * PAGE_SIZE
    win_buf = state_nsa_win.shape[1]

    c_all = jnp.concatenate([c_prompt, c_sample], axis=0)
    mod_all = _ada_mod(c_all, w_mod, b_mod).reshape(depth, bp + bs, N_MOD, d)
    mod_p = mod_all[:, :bp]
    mod_s = jnp.swapaxes(mod_all[:, bp:], 1, 2)
    modkv_all = _ada_mod(c_all, w_mod_kv[None], b_mod_kv[None]).reshape(bp + bs, 2, d)
    modkv_p = modkv_all[:bp]
    modkv_s = jnp.swapaxes(modkv_all[bp:], 0, 1)

    cos_p, sin_p, cos_pt, sin_pt = _rope_tables(np.arange(tp))
    cos_s, sin_s, _, _ = _rope_tables(np.asarray([past_len]))

    x_p = x_prompt
    x_s = x_sample.reshape(bs, d)
    lru_h_p, lru_h_s, lru_c_p, lru_c_s, ffn_c_p, ffn_c_s = [], [], [], [], [], []
    row1 = lambda a: a.reshape(1, -1)

    ctx_p = ctx_s = None
    for l in range(depth):
        g = g_norm[l]
        wi = w_ffn_in[l].astype(bf16)
        wo = w_ffn_out[l].astype(bf16)
        ffn_taps_s = jnp.swapaxes(state_ffn_conv[l], 0, 1)
        pre_p = pre_s = None
        if l < n_a:
            w_in = w_lru_in[l].astype(bf16)
            w_out = w_lru_out[l].astype(bf16)
            wg = _gate_blockdiag(w_lru_gate[l])
            args = (w_in, row1(b_lru_in[l]), w_lru_conv[l], row1(b_lru_conv[l]), wg, b_lru_gate[l], row1(lru_lambda[l]), w_out)
            x_p, h_last, cbuf = _lru_prompt(x_p, mod_p[l], g, *args)
            lru_h_p.append(h_last[:, 0])
            lru_c_p.append(cbuf[:, SUBLANES - (LRU_CONV_W - 1):])
            taps = jnp.swapaxes(state_lru_conv[l], 0, 1)
            x_s, h_new, up_new = _lru_sample(x_s, mod_s[l], g, state_lru_h[l], taps, *args)
            lru_h_s.append(h_new)
            lru_c_s.append(jnp.concatenate([state_lru_conv[l][:, 1:], up_new[:, None]], axis=1))
        else:
            j = l - n_a
            wq_p, wgt_p = _q_weights_prompt(w_nsa_qg[j])
            q_raw, q_rot, gates = _q_prompt(x_p, mod_p[l], g, wq_p, wgt_p, cos_pt, sin_pt)
            o_att = _attn_prompt(q_raw, q_rot, gates, *ctx_p)
            pre_p = (o_att, w_nsa_out[j].astype(bf16))
            wq_s, wgt_s = _q_weights_sample(w_nsa_qg[j])
            q_raw_s, q_rot_s, gates_s = _q_sample(x_s, mod_s[l], g, wq_s, wgt_s, cos_s, sin_s)
            o_s = _attn_sample(q_raw_s.reshape(bs, N_HEADS, KV_WIDTH), q_rot_s.reshape(bs, N_HEADS, KV_WIDTH),
                               gates_s.reshape(bs, N_HEADS, LANES), *ctx_s)
            pre_s = (o_s.reshape(bs, N_HEADS * KV_WIDTH), _out_weights_sample(w_nsa_out[j]))
        x_p, fbuf = _ffn_prompt(x_p, mod_p[l], g, wi, w_ffn_conv[l], row1(b_ffn_conv[l]), wo, pre=pre_p)
        ffn_c_p.append(fbuf[:, SUBLANES - (FFN_CONV_W - 1):])
        x_s, z_new = _ffn_sample(x_s, mod_s[l], g, ffn_taps_s, wi, w_ffn_conv[l], row1(b_ffn_conv[l]), wo, pre=pre_s)
        ffn_c_s.append(jnp.concatenate([state_ffn_conv[l][:, 1:], z_new[:, None]], axis=1))

        if l == n_a - 1:
            w_kv_b = w_kv.astype(bf16)
            w_vt = jnp.concatenate([w_kv[:, 3 * KV_WIDTH:4 * KV_WIDTH].T, w_kv[:, 5 * KV_WIDTH:6 * KV_WIDTH].T], axis=0).astype(bf16)
            rows_p, win_p, kaug, kwin, vst, vwt = _kv_prompt(x_p, modkv_p, row1(g_kv), w_kv_b, w_vt, cos_p, sin_p)
            kc_p, vct_p = _compress(rows_p, _compress_weights(pe_cmp, w_cmp1, b_cmp1, w_cmp2, True), True)
            c2st_p, nc_p, ns_p = _cmp_to_sel(tp, kc_p.shape[2], True)
            ctx_p = (kc_p, vct_p, c2st_p, kaug, vst, kwin, vwt, nc_p, ns_p)
            rows_s, win_s = _kv_sample(x_s, modkv_s, row1(g_kv), w_kv_b, cos_s, sin_s)
            pages = cache_nsa_kv.reshape(cache_nsa_kv.shape[0], PAGE_SIZE, 4 * KV_WIDTH)
            past = pages[page_table].reshape(bs, past_len, 4 * KV_WIDTH)
            kc_s, vc_s = _compress(past, _compress_weights(pe_cmp, w_cmp1, b_cmp1, w_cmp2, False), False)
            c2s_s, nc_s, ns_s = _cmp_to_sel(past_len + ts, kc_s.shape[1], False)
            ctx_s = (kc_s, vc_s, c2s_s, past, rows_s.reshape(bs, 1, 4 * KV_WIDTH),
                     state_nsa_win.reshape(bs, win_buf, 2 * KV_WIDTH), win_s.reshape(bs, 1, 2 * KV_WIDTH), nc_s, ns_s)

    keep_p = min(WINDOW, tp)
    kv_rows_p = rows_p.reshape(bp, tp, 4, N_KV_HEADS, HEAD_DIM)
    win_out_p = win_p[:, tp - keep_p:].reshape(bp, keep_p, 2, N_KV_HEADS, HEAD_DIM)
    kv_rows_s = rows_s.reshape(bs, ts, 4, N_KV_HEADS, HEAD_DIM)
    keep_s = min(WINDOW, past_len + ts)
    win_all = jnp.concatenate([state_nsa_win, win_s.reshape(bs, ts, 2, N_KV_HEADS, HEAD_DIM)], axis=1)
    win_out_s = win_all[:, win_buf + ts - keep_s:]
    return (x_p, x_s.reshape(bs, ts, d), kv_rows_p, kv_rows_s, win_out_p, win_out_s,
            jnp.stack(lru_h_p), jnp.stack(lru_h_s), jnp.stack(lru_c_p), jnp.stack(lru_c_s),
            jnp.stack(ffn_c_p), jnp.stack(ffn_c_s))
```

```python
import functools

import numpy as np
import jax
import jax.numpy as jnp
from jax import lax
from jax.experimental import pallas as pl
from jax.experimental.pallas import tpu as pltpu

f32 = jnp.float32
bf16 = jnp.bfloat16

D_MODEL = 1024
D_RNN = D_MODEL
LRU_HEADS = 8
LRU_BLOCK = D_RNN // LRU_HEADS
LRU_CONV_W = 4
LRU_C = 8.0
N_HEADS = 16
HEAD_DIM = 64
N_KV_HEADS = 4
GROUP = N_HEADS // N_KV_HEADS
Q_WIDTH = N_HEADS * HEAD_DIM
KV_WIDTH = N_KV_HEADS * HEAD_DIM
L_CMP = 32
D_CMP = 16
CMP_HIDDEN = 128
L_SEL = 64
TOP_N = 8
WINDOW = 512
ROT_DIM = HEAD_DIM // 4
ROPE_THETA = 500000.0
D_FF = 3 * D_MODEL
FFN_CONV_W = 3
N_MOD = 6
EPS = 1e-6
NEG_INF = -1e30
FORCED_SCORE = 1e6
PAGE_SIZE = 128

LANES = 128
SUBLANES = 8
VMEM_LIMIT = 60 * 1024 * 1024
ROW_TILE = 256
FFN_TILE = 512
Q_TILE = 512
K_CHUNK = 512
FF_CHUNK = 1024
BELOW_ALL = -3.0e38
SEQ_PER_STEP = 2
SEL_GROUP = 16


def _cparams(sem=None):
    return pltpu.CompilerParams(dimension_semantics=sem, vmem_limit_bytes=VMEM_LIMIT)


def _const_spec(shape):
    nd = len(shape)
    return pl.BlockSpec(shape, lambda *_: (0,) * nd, pipeline_mode=pl.Buffered(1))


def _rms(x, g):
    return x * lax.rsqrt(jnp.mean(x * x, axis=-1, keepdims=True) + EPS) * g


def _gelu(x):
    return 0.5 * x * (1.0 + jnp.tanh(0.7978845608028654 * (x + 0.044715 * (x * x * x))))


def _mm(a, w):
    return jnp.dot(a.astype(bf16), w, preferred_element_type=f32)


def _mm_nt(a, b):
    return lax.dot_general(a, b, (((1,), (1,)), ((), ())), preferred_element_type=f32)


def _mm_exact_rhs(a, b01):
    a1 = a.astype(bf16)
    r1 = a - a1.astype(f32)
    a2 = r1.astype(bf16)
    a3 = (r1 - a2.astype(f32)).astype(bf16)
    return (jnp.dot(a1, b01, preferred_element_type=f32) + jnp.dot(a2, b01, preferred_element_type=f32)
            + jnp.dot(a3, b01, preferred_element_type=f32))


def _mm_exact_lhs(a01, b):
    b1 = b.astype(bf16)
    r1 = b - b1.astype(f32)
    b2 = r1.astype(bf16)
    b3 = (r1 - b2.astype(f32)).astype(bf16)
    return (jnp.dot(a01, b1, preferred_element_type=f32) + jnp.dot(a01, b2, preferred_element_type=f32)
            + jnp.dot(a01, b3, preferred_element_type=f32))


def _shift_rows(cur, tail, k):
    r = pltpu.roll(cur, k, 0)
    rp = pltpu.roll(tail, k, 0)
    row = lax.broadcasted_iota(jnp.int32, tail.shape, 0)
    first = jnp.where(row < k, rp, r[0:SUBLANES])
    return jnp.concatenate([first, r[SUBLANES:]], axis=0)


def _causal_conv_tile(cur, tail, w, b):
    width = w.shape[0]
    out = b + w[width - 1:width] * cur
    for k in range(1, width):
        out = out + w[width - 1 - k:width - k] * _shift_rows(cur, tail, k)
    return out


def _lru_gates(u, wg_ref, bg, lam):
    pair = 2 * LRU_BLOCK
    n_pair = D_RNN // pair
    ub = u.astype(bf16)
    gates = []
    for gi in range(2):
        cols = [jnp.dot(ub[:, p * pair:(p + 1) * pair], wg_ref[gi, p], preferred_element_type=f32) for p in range(n_pair)]
        gates.append(jnp.concatenate(cols, axis=1) + bg[gi:gi + 1])
    r = jax.nn.sigmoid(gates[0])
    i = jax.nn.sigmoid(gates[1])
    z = -lam
    softplus = jnp.maximum(z, 0.0) + jnp.log1p(jnp.exp(-jnp.abs(z)))
    log_a = -LRU_C * r * softplus
    a = jnp.exp(log_a)
    mult = jnp.sqrt(-jnp.tanh(log_a) * (a * a + 1.0))
    return a, mult * (i * u)


def _scan_rows(a, b, h0):
    n, c = a.shape
    n_grp = n // SUBLANES
    a = a.reshape(n_grp, SUBLANES, c)
    b = b.reshape(n_grp, SUBLANES, c)
    row = lax.broadcasted_iota(jnp.int32, a.shape, 1)
    s = 1
    while s < SUBLANES:
        a_sh = jnp.where(row >= s, pltpu.roll(a, s, 1), 1.0)
        b_sh = jnp.where(row >= s, pltpu.roll(b, s, 1), 0.0)
        b = a * b_sh + b
        a = a * a_sh
        s *= 2
    out = []
    carry = h0
    for i in range(n_grp):
        h = b[i] + a[i] * carry
        out.append(h)
        carry = h[SUBLANES - 1:SUBLANES]
    return jnp.concatenate(out, axis=0)


def _rope_cols(x, cos_t, sin_t):
    lane = lax.broadcasted_iota(jnp.int32, x.shape, 1)
    half = ROT_DIM // 2
    swapped = jnp.where(lane % HEAD_DIM < half, pltpu.roll(x, LANES - half, 1), pltpu.roll(x, half, 1))
    return x * cos_t + swapped * sin_t


def _rope_wide(x, cos_t, sin_t):
    n = x.shape[1] // LANES
    return jnp.concatenate([_rope_cols(x[:, j * LANES:(j + 1) * LANES], cos_t, sin_t) for j in range(n)], axis=1)


def _topk_mask(imp, cur, n_blocks):
    blk = lax.broadcasted_iota(jnp.int32, imp.shape, 1)
    forced = (blk == 0) | (blk == cur) | (blk == cur - 1)
    imp = jnp.where(forced, FORCED_SCORE, imp)
    imp = jnp.where(blk <= cur, imp, NEG_INF)
    v = jnp.where(blk < n_blocks, imp, BELOW_ALL)
    rank = jnp.zeros(imp.shape, jnp.int32)
    for i in range(n_blocks):
        vi = v[:, i:i + 1]
        rank = rank + ((vi > v) | ((vi == v) & (blk > i))).astype(jnp.int32)
    return jnp.where((rank < TOP_N) & (v > 0.5 * NEG_INF), 1.0, 0.0)


def _mod_body(c_ref, w_ref, b_ref, o_ref):
    c = c_ref[...]
    o_ref[0] = _mm(c * jax.nn.sigmoid(c), w_ref[0].astype(bf16)) + b_ref[0]


def _ada_mod(c, w, b):
    n_l, d, n = w.shape
    m = c.shape[0]
    tn = n // 2 if n % (2 * LANES) == 0 else n
    return pl.pallas_call(
        _mod_body,
        grid=(n_l, n // tn),
        in_specs=[pl.BlockSpec((m, d), lambda l, j: (0, 0)),
                  pl.BlockSpec((1, d, tn), lambda l, j: (l, 0, j)),
                  pl.BlockSpec((1, 1, tn), lambda l, j: (l, 0, j))],
        out_specs=pl.BlockSpec((1, m, tn), lambda l, j: (l, 0, j)),
        out_shape=jax.ShapeDtypeStruct((n_l, m, n), f32),
        compiler_params=_cparams(("arbitrary", "arbitrary")),
        name="ada_mod",
    )(c, w, b.reshape(n_l, 1, n))


def _lru_prompt_body(x_ref, mod_ref, g_ref, win_ref, bin_ref, wc_ref, bc_ref, wg_ref, bg_ref, lam_ref, wout_ref,
                     xo_ref, hlast_ref, cbuf_ref, tail_ref, h_ref):
    t = pl.program_id(1)

    @pl.when(t == 0)
    def _():
        tail_ref[...] = jnp.zeros(tail_ref.shape, f32)
        h_ref[...] = jnp.zeros(h_ref.shape, f32)

    x = x_ref[0]
    tm = x.shape[0]
    mod = mod_ref[0]
    g = g_ref[...]
    h = _rms(x, g[0:1]) * (1.0 + mod[1:2]) + mod[0:1]
    proj = _mm(h, win_ref[...]) + bin_ref[...]
    y = _gelu(proj[:, :D_RNN])
    up = proj[:, D_RNN:]
    u = _causal_conv_tile(up, tail_ref[...], wc_ref[...], bc_ref[...])
    last_rows = up[tm - SUBLANES:tm]
    tail_ref[...] = last_rows
    cbuf_ref[0] = last_rows
    a, bx = _lru_gates(u, wg_ref, bg_ref[...], lam_ref[...])
    hs = _scan_rows(a, bx, h_ref[0:1])
    h_fin = hs[tm - 1:tm]
    h_ref[...] = jnp.broadcast_to(h_fin, h_ref.shape)
    hlast_ref[0] = jnp.broadcast_to(h_fin, hlast_ref.shape[1:])
    out = _mm(hs * y, wout_ref[...])
    xo_ref[0] = x + mod[2:3] * _rms(out, g[1:2])


def _lru_prompt(x, mod, g, w_in, b_in, w_conv, b_conv, wg, bg, lam, w_out):
    bsz, t, d = x.shape
    tm = min(ROW_TILE, t)
    row_spec = pl.BlockSpec((1, tm, d), lambda b, i: (b, i, 0))
    small = lambda r, c: pl.BlockSpec((1, r, c), lambda b, i: (b, 0, 0))
    return pl.pallas_call(
        _lru_prompt_body,
        grid=(bsz, t // tm),
        in_specs=[row_spec, small(N_MOD, d), _const_spec(g.shape), _const_spec(w_in.shape), _const_spec(b_in.shape),
                  _const_spec(w_conv.shape), _const_spec(b_conv.shape), _const_spec(wg.shape), _const_spec(bg.shape),
                  _const_spec(lam.shape), _const_spec(w_out.shape)],
        out_specs=[row_spec, small(SUBLANES, D_RNN), small(SUBLANES, D_RNN)],
        out_shape=[jax.ShapeDtypeStruct((bsz, t, d), f32), jax.ShapeDtypeStruct((bsz, SUBLANES, D_RNN), f32),
                   jax.ShapeDtypeStruct((bsz, SUBLANES, D_RNN), f32)],
        scratch_shapes=[pltpu.VMEM((SUBLANES, D_RNN), f32), pltpu.VMEM((SUBLANES, D_RNN), f32)],
        compiler_params=_cparams(("arbitrary", "arbitrary")),
        name="lru_prompt",
    )(x, mod, g, w_in, b_in, w_conv, b_conv, wg, bg, lam, w_out)


def _lru_sample_body(x_ref, mod_ref, g_ref, h0_ref, taps_ref, win_ref, bin_ref, wc_ref, bc_ref, wg_ref, bg_ref, lam_ref,
                     wout_ref, xo_ref, hnew_ref, up_ref):
    x = x_ref[...]
    g = g_ref[...]
    h = _rms(x, g[0:1]) * (1.0 + mod_ref[1]) + mod_ref[0]
    proj = _mm(h, win_ref[...]) + bin_ref[...]
    y = _gelu(proj[:, :D_RNN])
    up = proj[:, D_RNN:]
    up_ref[...] = up
    wc = wc_ref[...]
    u = bc_ref[...] + wc[LRU_CONV_W - 1:LRU_CONV_W] * up
    for k in range(LRU_CONV_W - 1):
        u = u + wc[k:k + 1] * taps_ref[k]
    a, bx = _lru_gates(u, wg_ref, bg_ref[...], lam_ref[...])
    hs = a * h0_ref[...] + bx
    hnew_ref[...] = hs
    out = _mm(hs * y, wout_ref[...])
    xo_ref[...] = x + mod_ref[2] * _rms(out, g[1:2])


def _lru_sample(x, mod, g, h0, taps, w_in, b_in, w_conv, b_conv, wg, bg, lam, w_out):
    m, d = x.shape
    return pl.pallas_call(
        _lru_sample_body,
        out_shape=[jax.ShapeDtypeStruct((m, d), f32), jax.ShapeDtypeStruct((m, D_RNN), f32),
                   jax.ShapeDtypeStruct((m, D_RNN), f32)],
        compiler_params=_cparams(),
        name="lru_sample",
    )(x, mod, g, h0, taps, w_in, b_in, w_conv, b_conv, wg, bg, lam, w_out)


def _ffn_core(h, tap_fn, wi_ref, wc, bc, wo_ref, store_fn):
    acc = jnp.zeros((h.shape[0], D_MODEL), f32)
    hb = h.astype(bf16)
    for c in range(D_FF // FF_CHUNK):
        zs = []
        for base in (0, D_FF):
            lo = base + c * FF_CHUNK
            z = jnp.dot(hb, wi_ref[:, lo:lo + FF_CHUNK], preferred_element_type=f32)
            store_fn(z, lo)
            zc = bc[:, lo:lo + FF_CHUNK] + wc[FFN_CONV_W - 1:FFN_CONV_W, lo:lo + FF_CHUNK] * z
            for k in range(1, FFN_CONV_W):
                zc = zc + wc[FFN_CONV_W - 1 - k:FFN_CONV_W - k, lo:lo + FF_CHUNK] * tap_fn(z, lo, k)
            zs.append(zc)
        gz = _gelu(zs[0]) * zs[1]
        acc = acc + jnp.dot(gz.astype(bf16), wo_ref[c * FF_CHUNK:(c + 1) * FF_CHUNK, :], preferred_element_type=f32)
    return acc


def _ffn_prompt_body(has_pre, *refs):
    if has_pre:
        (x_ref, o_ref, wpre_ref, mod_ref, g_ref, wi_ref, wc_ref, bc_ref, wo_ref, xo_ref, fbuf_ref, tail_ref) = refs
    else:
        (x_ref, mod_ref, g_ref, wi_ref, wc_ref, bc_ref, wo_ref, xo_ref, fbuf_ref, tail_ref) = refs
    t = pl.program_id(1)

    @pl.when(t == 0)
    def _():
        tail_ref[...] = jnp.zeros(tail_ref.shape, f32)

    x = x_ref[0]
    tm = x.shape[0]
    mod = mod_ref[0]
    g = g_ref[...]
    if has_pre:
        x = x + mod[2:3] * _rms(jnp.dot(o_ref[0], wpre_ref[...], preferred_element_type=f32), g[1:2])
    h = _rms(x, g[2:3]) * (1.0 + mod[4:5]) + mod[3:4]

    def tap(z, lo, k):
        return _shift_rows(z, tail_ref[:, lo:lo + FF_CHUNK], k)

    def store(z, lo):
        fbuf_ref[0, :, lo:lo + FF_CHUNK] = z[tm - SUBLANES:tm]

    acc = _ffn_core(h, tap, wi_ref, wc_ref[...], bc_ref[...], wo_ref, store)
    tail_ref[...] = fbuf_ref[0]
    xo_ref[0] = x + mod[5:6] * _rms(acc, g[3:4])


def _ffn_prompt(x, mod, g, w_in, w_conv, b_conv, w_out, pre=None):
    bsz, t, d = x.shape
    tm = min(FFN_TILE, t)
    row_spec = pl.BlockSpec((1, tm, d), lambda b, i: (b, i, 0))
    small = lambda r, c: pl.BlockSpec((1, r, c), lambda b, i: (b, 0, 0))
    ins, specs = [x], [row_spec]
    if pre is not None:
        o_att, w_pre = pre
        ins += [o_att, w_pre]
        specs += [pl.BlockSpec((1, tm, o_att.shape[2]), lambda b, i: (b, i, 0)), _const_spec(w_pre.shape)]
    ins += [mod, g, w_in, w_conv, b_conv, w_out]
    specs += [small(N_MOD, d), _const_spec(g.shape), _const_spec(w_in.shape), _const_spec(w_conv.shape),
              _const_spec(b_conv.shape), _const_spec(w_out.shape)]
    return pl.pallas_call(
        functools.partial(_ffn_prompt_body, pre is not None),
        grid=(bsz, t // tm),
        in_specs=specs,
        out_specs=[row_spec, small(SUBLANES, 2 * D_FF)],
        out_shape=[jax.ShapeDtypeStruct((bsz, t, d), f32), jax.ShapeDtypeStruct((bsz, SUBLANES, 2 * D_FF), f32)],
        scratch_shapes=[pltpu.VMEM((SUBLANES, 2 * D_FF), f32)],
        compiler_params=_cparams(("arbitrary", "arbitrary")),
        name="ffn_prompt",
    )(*ins)


def _ffn_sample_body(has_pre, *refs):
    if has_pre:
        (x_ref, o_ref, wpre_ref, mod_ref, g_ref, taps_ref, wi_ref, wc_ref, bc_ref, wo_ref, xo_ref, z_ref) = refs
    else:
        (x_ref, mod_ref, g_ref, taps_ref, wi_ref, wc_ref, bc_ref, wo_ref, xo_ref, z_ref) = refs
    x = x_ref[...]
    g = g_ref[...]
    if has_pre:
        x = x + mod_ref[2] * _rms(jnp.dot(o_ref[...], wpre_ref[...], preferred_element_type=f32), g[1:2])
    h = _rms(x, g[2:3]) * (1.0 + mod_ref[4]) + mod_ref[3]

    def tap(z, lo, k):
        return taps_ref[FFN_CONV_W - 1 - k, :, lo:lo + FF_CHUNK]

    def store(z, lo):
        z_ref[:, lo:lo + FF_CHUNK] = z

    acc = _ffn_core(h, tap, wi_ref, wc_ref[...], bc_ref[...], wo_ref, store)
    xo_ref[...] = x + mod_ref[5] * _rms(acc, g[3:4])


def _ffn_sample(x, mod, g, taps, w_in, w_conv, b_conv, w_out, pre=None):
    m, d = x.shape
    ins = [x] + (list(pre) if pre is not None else []) + [mod, g, taps, w_in, w_conv, b_conv, w_out]
    return pl.pallas_call(
        functools.partial(_ffn_sample_body, pre is not None),
        out_shape=[jax.ShapeDtypeStruct((m, d), f32), jax.ShapeDtypeStruct((m, 2 * D_FF), f32)],
        compiler_params=_cparams(),
        name="ffn_sample",
    )(*ins)


def _head_cols(src, extra):
    lane = lax.broadcasted_iota(jnp.int32, (src.shape[0], LANES), 1)
    out = []
    for j in range(KV_WIDTH // LANES):
        col = src[:, j * LANES:(j + 1) * LANES]
        out.append(jnp.where(lane < HEAD_DIM, col, extra))
        out.append(jnp.where(lane < HEAD_DIM, pltpu.roll(col, HEAD_DIM, 1), extra))
    return out


def _kv_prompt_body(x_ref, mod_ref, g_ref, w_ref, wvt_ref, cos_ref, sin_ref, rows_ref, win_ref, kaug_ref, kwin_ref, vst_ref, vwt_ref):
    x = x_ref[0]
    tm = x.shape[0]
    mod = mod_ref[0]
    h = _rms(x, g_ref[...]) * (1.0 + mod[1:2]) + mod[0:1]
    hb = h.astype(bf16)
    kv = jnp.dot(hb, w_ref[...], preferred_element_type=f32)
    vt = _mm_nt(wvt_ref[...], hb)
    cos_t, sin_t = cos_ref[...], sin_ref[...]
    w = KV_WIDTH
    k_sel = _rope_wide(kv[:, 2 * w:3 * w], cos_t, sin_t)
    k_win = _rope_wide(kv[:, 4 * w:5 * w], cos_t, sin_t)
    rows_ref[0, :, 0:2 * w] = kv[:, 0:2 * w]
    rows_ref[0, :, 2 * w:3 * w] = k_sel
    rows_ref[0, :, 3 * w:4 * w] = kv[:, 3 * w:4 * w]
    win_ref[0, :, 0:w] = k_win
    win_ref[0, :, w:2 * w] = kv[:, 5 * w:6 * w]
    row = lax.broadcasted_iota(jnp.int32, (tm, LANES), 0)
    lane = lax.broadcasted_iota(jnp.int32, (tm, LANES), 1)
    pos = pl.program_id(1) * tm + row
    onehot = jnp.where(lane - HEAD_DIM == (pos // L_SEL) % SEL_GROUP, 1.0, 0.0)
    for gi, col in enumerate(_head_cols(k_sel, onehot)):
        for c in range(tm // K_CHUNK):
            kaug_ref[0, gi, c] = col[c * K_CHUNK:(c + 1) * K_CHUNK].astype(bf16)
    for gi, col in enumerate(_head_cols(k_win, 0.0)):
        for c in range(tm // Q_TILE):
            kwin_ref[0, gi, c] = col[c * Q_TILE:(c + 1) * Q_TILE, 0:HEAD_DIM].astype(bf16)
    for gi in range(N_KV_HEADS):
        v_s = vt[gi * HEAD_DIM:(gi + 1) * HEAD_DIM].astype(bf16)
        v_w = vt[w + gi * HEAD_DIM:w + (gi + 1) * HEAD_DIM].astype(bf16)
        for c in range(tm // K_CHUNK):
            vst_ref[0, gi, c] = v_s[:, c * K_CHUNK:(c + 1) * K_CHUNK]
        for c in range(tm // Q_TILE):
            vwt_ref[0, gi, c] = v_w[:, c * Q_TILE:(c + 1) * Q_TILE]


def _kv_prompt(x, mod, g_kv, w_kv, w_vt, cos_t, sin_t):
    bsz, t, d = x.shape
    tm = K_CHUNK
    nc, nt = tm // K_CHUNK, tm // Q_TILE
    row = lambda c: pl.BlockSpec((1, tm, c), lambda b, i: (b, i, 0))
    chunked = lambda n, r, c: pl.BlockSpec((1, N_KV_HEADS, n, r, c), lambda b, i: (b, 0, i, 0, 0))
    shape5 = lambda n, r, c: jax.ShapeDtypeStruct((bsz, N_KV_HEADS, n, r, c), bf16)
    return pl.pallas_call(
        _kv_prompt_body,
        grid=(bsz, t // tm),
        in_specs=[row(d), pl.BlockSpec((1, 2, d), lambda b, i: (b, 0, 0)), _const_spec(g_kv.shape), _const_spec(w_kv.shape),
                  _const_spec(w_vt.shape), pl.BlockSpec((tm, LANES), lambda b, i: (i, 0)),
                  pl.BlockSpec((tm, LANES), lambda b, i: (i, 0))],
        out_specs=[row(4 * KV_WIDTH), row(2 * KV_WIDTH), chunked(nc, K_CHUNK, LANES), chunked(nt, Q_TILE, HEAD_DIM),
                   chunked(nc, HEAD_DIM, K_CHUNK), chunked(nt, HEAD_DIM, Q_TILE)],
        out_shape=[jax.ShapeDtypeStruct((bsz, t, 4 * KV_WIDTH), f32), jax.ShapeDtypeStruct((bsz, t, 2 * KV_WIDTH), f32),
                   shape5(t // K_CHUNK, K_CHUNK, LANES), shape5(t // Q_TILE, Q_TILE, HEAD_DIM),
                   shape5(t // K_CHUNK, HEAD_DIM, K_CHUNK), shape5(t // Q_TILE, HEAD_DIM, Q_TILE)],
        compiler_params=_cparams(("arbitrary", "arbitrary")),
        name="kv_prompt",
    )(x, mod, g_kv, w_kv, w_vt, cos_t, sin_t)


def _kv_sample_body(x_ref, mod_ref, g_ref, w_ref, cos_ref, sin_ref, rows_ref, win_ref):
    h = _rms(x_ref[...], g_ref[...]) * (1.0 + mod_ref[1]) + mod_ref[0]
    kv = _mm(h, w_ref[...])
    cos_t, sin_t = cos_ref[...], sin_ref[...]
    w = KV_WIDTH
    rows_ref[:, 0:2 * w] = kv[:, 0:2 * w]
    rows_ref[:, 2 * w:3 * w] = _rope_wide(kv[:, 2 * w:3 * w], cos_t, sin_t)
    rows_ref[:, 3 * w:4 * w] = kv[:, 3 * w:4 * w]
    win_ref[:, 0:w] = _rope_wide(kv[:, 4 * w:5 * w], cos_t, sin_t)
    win_ref[:, w:2 * w] = kv[:, 5 * w:6 * w]


def _kv_sample(x, mod, g_kv, w_kv, cos_t, sin_t):
    m = x.shape[0]
    return pl.pallas_call(
        _kv_sample_body,
        out_shape=[jax.ShapeDtypeStruct((m, 4 * KV_WIDTH), f32), jax.ShapeDtypeStruct((m, 2 * KV_WIDTH), f32)],
        compiler_params=_cparams(),
        name="kv_sample",
    )(x, mod, g_kv, w_kv, cos_t, sin_t)


def _compress_body(per_group, n_pages, *refs):
    n_slab = 2 * KV_WIDTH // LANES
    if n_pages:
        pages = refs[1:1 + n_pages]
        pe_ref, w1_ref, wp_ref, b1_ref, w2_ref = refs[1 + n_pages:6 + n_pages]
        outs = refs[6 + n_pages:8 + n_pages]
        slab_ref = refs[8 + n_pages]
        for p, page in enumerate(pages):
            for j in range(n_slab):
                slab_ref[j, p * PAGE_SIZE:(p + 1) * PAGE_SIZE, :] = page[0, :, j * LANES:(j + 1) * LANES]
        n_chunk = n_pages * PAGE_SIZE // D_CMP
        rows_of = lambda j, start: slab_ref[j, pl.ds(start, n_chunk, stride=D_CMP), :]
    else:
        slabs = refs[:n_slab]
        pe_ref, w1_ref, wp_ref, b1_ref, w2_ref = refs[n_slab:n_slab + 5]
        outs = refs[n_slab + 5:]
        n_chunk = slabs[0].shape[1] // D_CMP
        rows_of = lambda j, start: slabs[j][0, pl.ds(start, n_chunk, stride=D_CMP), :]
    per_slot = KV_WIDTH // LANES
    hid = CMP_HIDDEN
    for s in range(2):
        const = _mm(jnp.broadcast_to(pe_ref[s], (SUBLANES, L_CMP * HEAD_DIM)), w1_ref[s])[0:1] + b1_ref[s]
        tok = jnp.zeros((n_chunk, KV_WIDTH), f32)
        for j in range(per_slot):
            acc = jnp.zeros((n_chunk, 4 * hid), f32)
            for rp in range(D_CMP // 2):
                lhs = jnp.concatenate([rows_of(s * per_slot + j, 2 * rp), rows_of(s * per_slot + j, 2 * rp + 1)], axis=1)
                acc = acc + _mm(lhs, wp_ref[s, rp])
            for hl in range(2):
                gi = 2 * j + hl
                first = acc[:, 2 * hl * hid:(2 * hl + 1) * hid]
                second = acc[:, (2 * hl + 1) * hid:(2 * hl + 2) * hid]
                act_g = _gelu(first + pltpu.roll(second, n_chunk - 1, 0) + const).astype(bf16)
                if not per_group:
                    tok = tok + jnp.dot(act_g, w2_ref[s, gi], preferred_element_type=f32)
                elif s == 0:
                    outs[s][0, gi] = jnp.dot(act_g, w2_ref[s], preferred_element_type=f32)[:, 0:HEAD_DIM].astype(bf16)
                else:
                    outs[s][0, gi] = _mm_nt(w2_ref[s], act_g)[0:HEAD_DIM].astype(bf16)
        if not per_group:
            outs[s][0] = tok.astype(bf16)


def _compress(rows, cw, per_group):
    bsz, t = rows.shape[:2]
    n_chunk = t // D_CMP
    n_slab = 2 * KV_WIDTH // LANES
    slab_specs = [pl.BlockSpec((1, t, LANES), functools.partial(lambda j, b: (b, 0, j), j)) for j in range(n_slab)]
    pe, wt, wb, b1, w2 = cw
    if per_group:
        dims = [(n_chunk, HEAD_DIM), (HEAD_DIM, n_chunk)]
        out_specs = [pl.BlockSpec((1, N_KV_HEADS) + dm, lambda b: (b, 0, 0, 0)) for dm in dims]
        out_shapes = [jax.ShapeDtypeStruct((bsz, N_KV_HEADS) + dm, bf16) for dm in dims]
    else:
        out_specs = [pl.BlockSpec((1, n_chunk, KV_WIDTH), lambda b: (b, 0, 0))] * 2
        out_shapes = [jax.ShapeDtypeStruct((bsz, n_chunk, KV_WIDTH), bf16)] * 2
    return pl.pallas_call(
        functools.partial(_compress_body, per_group, 0),
        grid=(bsz,),
        in_specs=slab_specs + [_const_spec(pe.shape), _const_spec(wt.shape), _const_spec(wb.shape), _const_spec(b1.shape),
                               _const_spec(w2.shape)],
        out_specs=out_specs,
        out_shape=out_shapes,
        compiler_params=_cparams(("arbitrary",)),
        name="compress_prompt" if per_group else "compress_sample",
    )(*([rows] * n_slab), pe, wt, wb, b1, w2)


def _page_specs(n_seq, n_pages, lane_block):
    def spec(i, p):
        return pl.BlockSpec((1, PAGE_SIZE, 2 * KV_WIDTH), lambda b, pt: (pt[(b * n_seq + i) * n_pages + p], 0, lane_block))
    return [spec(i, p) for i in range(n_seq) for p in range(n_pages)]


def _compress_paged(pages, page_table, cw):
    bsz, n_pages = page_table.shape
    t = n_pages * PAGE_SIZE
    n_chunk = t // D_CMP
    out_spec = pl.BlockSpec((1, n_chunk, KV_WIDTH), lambda b, pt: (b, 0, 0))
    out_shape = jax.ShapeDtypeStruct((bsz, n_chunk, KV_WIDTH), bf16)
    return pl.pallas_call(
        functools.partial(_compress_body, False, n_pages),
        grid_spec=pltpu.PrefetchScalarGridSpec(
            num_scalar_prefetch=1,
            grid=(bsz,),
            in_specs=_page_specs(1, n_pages, 0) + [_const_spec(a.shape) for a in cw],
            out_specs=[out_spec, out_spec],
            scratch_shapes=[pltpu.VMEM((2 * KV_WIDTH // LANES, t, LANES), f32)]),
        out_shape=[out_shape, out_shape],
        compiler_params=_cparams(("arbitrary",)),
        name="compress_sample",
    )(page_table.reshape(-1), *([pages] * n_pages), *cw)


def _q_prompt_body(x_ref, mod_ref, g_ref, wqt_ref, wgt_ref, cos_ref, sin_ref, qraw_ref, qrot_ref, gate_ref):
    mod = mod_ref[0]
    h = _rms(x_ref[0], g_ref[0:1]) * (1.0 + mod[1:2]) + mod[0:1]
    hb = h.astype(bf16)
    qt = _mm_nt(wqt_ref[...], hb)
    gate_ref[0] = jax.nn.sigmoid(_mm_nt(wgt_ref[...], hb))
    scale = HEAD_DIM ** -0.5
    qraw_ref[0] = (qt * scale).astype(bf16)
    half = ROT_DIM // 2
    dim = lax.broadcasted_iota(jnp.int32, qt.shape, 0) % HEAD_DIM
    swapped = jnp.where(dim < half, pltpu.roll(qt, Q_WIDTH - half, 0), pltpu.roll(qt, half, 0))
    cos_t = jnp.concatenate([cos_ref[...]] * N_HEADS, axis=0)
    sin_t = jnp.concatenate([sin_ref[...]] * N_HEADS, axis=0)
    qrot_ref[0] = ((qt * cos_t + swapped * sin_t) * scale).astype(bf16)


def _q_prompt(x, mod, g, wqt, wgt, cos_tt, sin_tt):
    bsz, t, d = x.shape
    tm = min(ROW_TILE, t)
    col = lambda r: pl.BlockSpec((1, r, tm), lambda b, i: (b, 0, i))
    tab = pl.BlockSpec((HEAD_DIM, tm), lambda b, i: (0, i))
    return pl.pallas_call(
        _q_prompt_body,
        grid=(bsz, t // tm),
        in_specs=[pl.BlockSpec((1, tm, d), lambda b, i: (b, i, 0)), pl.BlockSpec((1, N_MOD, d), lambda b, i: (b, 0, 0)),
                  _const_spec(g.shape), _const_spec(wqt.shape), _const_spec(wgt.shape), tab, tab],
        out_specs=[col(Q_WIDTH), col(Q_WIDTH), col(wgt.shape[0])],
        out_shape=[jax.ShapeDtypeStruct((bsz, Q_WIDTH, t), bf16), jax.ShapeDtypeStruct((bsz, Q_WIDTH, t), bf16),
                   jax.ShapeDtypeStruct((bsz, wgt.shape[0], t), f32)],
        compiler_params=_cparams(("arbitrary", "arbitrary")),
        name="q_prompt",
    )(x, mod, g, wqt, wgt, cos_tt, sin_tt)


def _q_sample_body(x_ref, mod_ref, g_ref, wq_ref, wgt_ref, cos_ref, sin_ref, qraw_ref, qrot_ref, gate_ref):
    h = _rms(x_ref[...], g_ref[0:1]) * (1.0 + mod_ref[1]) + mod_ref[0]
    hb = h.astype(bf16)
    q = jnp.dot(hb, wq_ref[...], preferred_element_type=f32)
    gate_ref[...] = jax.nn.sigmoid(jnp.dot(hb, wgt_ref[...], preferred_element_type=f32))
    scale = HEAD_DIM ** -0.5
    qraw_ref[...] = (q * scale).astype(bf16)
    qrot_ref[...] = (_rope_wide(q, cos_ref[...], sin_ref[...]) * scale).astype(bf16)


def _q_sample(x, mod, g, wq, wgt, cos_t, sin_t):
    m = x.shape[0]
    return pl.pallas_call(
        _q_sample_body,
        out_shape=[jax.ShapeDtypeStruct((m, wq.shape[1]), bf16), jax.ShapeDtypeStruct((m, wq.shape[1]), bf16),
                   jax.ShapeDtypeStruct((m, wgt.shape[1]), f32)],
        compiler_params=_cparams(),
        name="q_sample",
    )(x, mod, g, wq, wgt, cos_t, sin_t)


def _rank_select(imp, cur, n_blocks):
    blk = lax.broadcasted_iota(jnp.int32, imp.shape, 0)
    forced = (blk == 0) | (blk == cur) | (blk == cur - 1)
    v = jnp.where(forced, FORCED_SCORE, imp)
    v = jnp.where(blk <= cur, v, NEG_INF)
    v = jnp.where(blk < n_blocks, v, BELOW_ALL)
    rank = jnp.zeros(imp.shape, jnp.int32)
    for i in range(n_blocks):
        vi = v[i:i + 1, :]
        rank = rank + ((vi > v) | ((vi == v) & (blk > i))).astype(jnp.int32)
    return (rank < TOP_N) & (v > 0.5 * NEG_INF)


def _attn_prompt_body(n_cmp, n_sel, qraw_ref, qrot_ref, gate_ref, kc_ref, vct_ref, c2st_ref, kaug_ref, vst_ref, kwin_ref,
                      vwt_ref, o_ref, qaug_ref, negsel_ref):
    qi = pl.program_id(2)
    tq = qraw_ref.shape[2]
    n4 = GROUP * tq
    q_chunk = (qi * tq) // K_CHUNK

    def stack(ref):
        return jnp.concatenate([ref[0, r * HEAD_DIM:(r + 1) * HEAD_DIM, :] for r in range(GROUP)], axis=1)

    q_raw = stack(qraw_ref)
    q_rot = stack(qrot_ref)
    q_pos = qi * tq + lax.broadcasted_iota(jnp.int32, (1, n4), 1) % tq

    s = jnp.dot(kc_ref[0, 0], q_raw, preferred_element_type=f32)
    blk = lax.broadcasted_iota(jnp.int32, s.shape, 0)
    mask = (blk * D_CMP + (L_CMP - 1) <= q_pos) & (blk < n_cmp)
    s = jnp.where(mask, s, NEG_INF)
    p = jnp.where(mask, jnp.exp(s - jnp.max(s, axis=0, keepdims=True)), 0.0)
    l = jnp.sum(p, axis=0, keepdims=True)
    p = p / jnp.where(l > 0.0, l, 1.0)
    o_c = jnp.dot(vct_ref[0, 0], p.astype(bf16), preferred_element_type=f32)

    p_group = p[:, 0:tq]
    for r in range(1, GROUP):
        p_group = p_group + p[:, r * tq:(r + 1) * tq]
    imp = _mm_exact_lhs(c2st_ref[...], p_group)
    n_rows = negsel_ref.shape[0]
    picked = _rank_select(imp[0:n_rows], q_pos[:, 0:tq] // L_SEL, n_sel)
    negsel = jnp.where(picked, 0.0, NEG_INF)
    negsel_ref[...] = jnp.concatenate([negsel] * GROUP, axis=1)

    qaug_ref[0:HEAD_DIM, :] = q_rot
    qaug_ref[HEAD_DIM + SEL_GROUP:, :] = jnp.zeros((LANES - HEAD_DIM - SEL_GROUP, n4), bf16)
    blocks_per_chunk = K_CHUNK // L_SEL

    def chunk_scores(c):
        first = pl.multiple_of((c * blocks_per_chunk // SEL_GROUP) * SEL_GROUP, SEL_GROUP)
        qaug_ref[HEAD_DIM:HEAD_DIM + SEL_GROUP, :] = negsel_ref[pl.ds(first, SEL_GROUP), :].astype(bf16)
        return jnp.dot(kaug_ref[0, 0, c], qaug_ref[...], preferred_element_type=f32)

    def update(sc, carry, vt):
        m_old, l_old, acc_old = carry
        m_new = jnp.maximum(m_old, jnp.max(sc, axis=0, keepdims=True))
        alpha = jnp.exp(m_old - m_new)
        pc = jnp.exp(sc - m_new)
        l_new = alpha * l_old + jnp.sum(pc, axis=0, keepdims=True)
        acc_new = alpha * acc_old + jnp.dot(vt, pc.astype(bf16), preferred_element_type=f32)
        return m_new, l_new, acc_new

    init = (jnp.full((1, n4), NEG_INF, f32), jnp.zeros((1, n4), f32), jnp.zeros((HEAD_DIM, n4), f32))
    carry = lax.fori_loop(0, q_chunk, lambda c, cr: update(chunk_scores(c), cr, vst_ref[0, 0, c]), init)
    sc = chunk_scores(q_chunk)
    key_pos = q_chunk * K_CHUNK + lax.broadcasted_iota(jnp.int32, sc.shape, 0)
    sc = jnp.where(key_pos <= q_pos, sc, NEG_INF)
    _, l_s, acc_s = update(sc, carry, vst_ref[0, 0, q_chunk])

    n_back = WINDOW // tq
    row = lax.broadcasted_iota(jnp.int32, (tq, n4), 0)
    col = lax.broadcasted_iota(jnp.int32, (tq, n4), 1) % tq
    tiles, vts = [], []
    for j in range(n_back + 1):
        ti = qi - n_back + j
        tic = jnp.maximum(ti, 0)
        sw = jnp.dot(kwin_ref[0, 0, tic], q_rot, preferred_element_type=f32)
        if j < n_back:
            sw = sw + jnp.where(ti >= 0, 0.0, NEG_INF)
        if j == 0:
            sw = jnp.where(row > col, sw, NEG_INF)
        elif j == n_back:
            sw = jnp.where(row <= col, sw, NEG_INF)
        tiles.append(sw)
        vts.append(vwt_ref[0, 0, tic])
    sw = jnp.concatenate(tiles, axis=0)
    pw = jnp.exp(sw - jnp.max(sw, axis=0, keepdims=True))
    l_w = jnp.sum(pw, axis=0, keepdims=True)
    acc_w = jnp.dot(jnp.concatenate(vts, axis=1), pw.astype(bf16), preferred_element_type=f32)

    gate = gate_ref[0]
    o_s = acc_s / l_s
    o_w = acc_w / l_w
    heads = []
    for r in range(GROUP):
        sl = slice(r * tq, (r + 1) * tq)
        heads.append(gate[3 * r:3 * r + 1] * o_c[:, sl] + gate[3 * r + 1:3 * r + 2] * o_s[:, sl]
                     + gate[3 * r + 2:3 * r + 3] * o_w[:, sl])
    o_ref[0] = jnp.concatenate(heads, axis=0).T.astype(o_ref.dtype)


def _attn_prompt(qt_raw, qt_rot, gates_t, kc, vct, c2st, kaug, vst, kwin, vwt, n_cmp, n_sel):
    bsz, _, t = qt_raw.shape
    tq = Q_TILE
    gw = GROUP * HEAD_DIM
    n4 = GROUP * tq
    q_spec = pl.BlockSpec((1, gw, tq), lambda b, g, i: (b, g, i))
    gate_rows = gates_t.shape[1] // N_KV_HEADS
    per_group = lambda a: pl.BlockSpec((1, 1) + a.shape[2:], lambda b, g, i: (b, g) + (0,) * (a.ndim - 2))
    sel_rows = -(-n_sel // SEL_GROUP) * SEL_GROUP
    return pl.pallas_call(
        functools.partial(_attn_prompt_body, n_cmp, n_sel),
        grid=(bsz, N_KV_HEADS, t // tq),
        in_specs=[q_spec, q_spec, pl.BlockSpec((1, gate_rows, tq), lambda b, g, i: (b, g, i)), per_group(kc), per_group(vct),
                  pl.BlockSpec(c2st.shape, lambda b, g, i: (0, 0)), per_group(kaug), per_group(vst), per_group(kwin),
                  per_group(vwt)],
        out_specs=pl.BlockSpec((1, tq, gw), lambda b, g, i: (b, i, g)),
        out_shape=jax.ShapeDtypeStruct((bsz, t, Q_WIDTH), bf16),
        scratch_shapes=[pltpu.VMEM((LANES, n4), bf16), pltpu.VMEM((sel_rows, n4), f32)],
        compiler_params=_cparams(("arbitrary", "arbitrary", "arbitrary")),
        name="attn_prompt",
    )(qt_raw, qt_rot, gates_t, kc, vct, c2st, kaug, vst, kwin, vwt)


def _attn_sample_body(n_cmp, n_sel, win_buf, emit_win, n_pages, *refs):
    fixed, rest = refs[1:10], refs[10:]
    n_seq = fixed[0].shape[0]
    pages, outs = rest[:n_seq * n_pages], rest[n_seq * n_pages:]
    for i in range(n_seq):
        _attn_sample_one(n_cmp, n_sel, win_buf, emit_win, i, pages[i * n_pages:(i + 1) * n_pages], *fixed, *outs)


def _attn_sample_one(n_cmp, n_sel, win_buf, emit_win, i, pages, qraw_ref, qrot_ref, gate_ref, kc_ref, vc_ref, c2s_ref,
                     new_ref, wstate_ref, wnew_ref, o_ref, *win_out):
    w = KV_WIDTH
    q_raw = qraw_ref[i]
    q_rot = qrot_ref[i]
    n_head = q_raw.shape[0]
    head = lax.broadcasted_iota(jnp.int32, (n_head, n_head), 0)
    head2 = lax.broadcasted_iota(jnp.int32, (n_head, n_head), 1)
    same_group = (head // GROUP == head2 // GROUP).astype(bf16)

    s = _mm_nt(q_raw, kc_ref[i])
    blk = lax.broadcasted_iota(jnp.int32, s.shape, 1)
    mask = blk < n_cmp
    s = jnp.where(mask, s, NEG_INF)
    p = jnp.where(mask, jnp.exp(s - jnp.max(s, axis=1, keepdims=True)), 0.0)
    p = p / jnp.sum(p, axis=1, keepdims=True)
    o_c = jnp.dot(p.astype(bf16), vc_ref[i], preferred_element_type=f32)
    imp = _mm_exact_rhs(_mm_exact_lhs(same_group, p), c2s_ref[...])
    cur = jnp.full((n_head, 1), n_sel - 1, jnp.int32)
    sel = _topk_mask(imp, cur, n_sel)

    def attend(q, parts, mask_past, k_new, v_new, new_ok):
        sp = jnp.concatenate([_mm_nt(q, ref[j, :, 0:w].astype(bf16)) for ref, j in parts], axis=1)
        sp = jnp.where(mask_past, sp, NEG_INF)
        sn = jnp.sum(q.astype(f32) * k_new.astype(bf16).astype(f32), axis=1, keepdims=True)
        sn = jnp.where(new_ok, sn, NEG_INF)
        m = jnp.maximum(jnp.max(sp, axis=1, keepdims=True), sn)
        pp = jnp.where(mask_past, jnp.exp(sp - m), 0.0)
        pn = jnp.where(new_ok, jnp.exp(sn - m), 0.0)
        l = jnp.sum(pp, axis=1, keepdims=True) + pn
        pp = pp.astype(bf16)
        acc = pn.astype(bf16).astype(f32) * v_new.astype(bf16).astype(f32)
        rows = 0
        for ref, j in parts:
            n = ref.shape[1]
            acc = acc + jnp.dot(pp[:, rows:rows + n], ref[j, :, w:2 * w].astype(bf16), preferred_element_type=f32)
            rows += n
        return acc / l

    n_past = len(pages) * PAGE_SIZE
    jj = lax.broadcasted_iota(jnp.int32, (LANES, n_past), 0)
    kk = lax.broadcasted_iota(jnp.int32, (LANES, n_past), 1)
    expand = (jj == kk // L_SEL).astype(bf16)
    picked = jnp.dot(sel.astype(bf16), expand, preferred_element_type=f32) > 0.5
    new_row = new_ref[i]
    o_s = attend(q_rot, [(pg, 0) for pg in pages], picked, new_row[:, 2 * w:3 * w], new_row[:, 3 * w:4 * w],
                 sel[:, n_sel - 1:n_sel] > 0.5)

    idx = lax.broadcasted_iota(jnp.int32, (n_head, win_buf), 1)
    in_window = win_buf - idx < WINDOW
    wnew = wnew_ref[i]
    o_w = attend(q_rot, [(wstate_ref, i)], in_window, wnew[:, 0:w], wnew[:, w:2 * w], jnp.full((n_head, 1), True))

    gate = gate_ref[i]
    o = gate[:, 0:1] * o_c + gate[:, 1:2] * o_s + gate[:, 2:3] * o_w
    hrow = lax.broadcasted_iota(jnp.int32, o.shape, 0)
    lane = lax.broadcasted_iota(jnp.int32, o.shape, 1)
    o_ref[i] = jnp.where(lane // HEAD_DIM == hrow // GROUP, o, 0.0).astype(o_ref.dtype)

    if emit_win:
        state = wstate_ref[i]
        row = lax.broadcasted_iota(jnp.int32, state.shape, 0)
        win_out[0][i] = jnp.where(row == win_buf - 1, wnew, pltpu.roll(state, win_buf - 1, 0))


def _attn_sample(q_raw, q_rot, gates, kc, vc, c2s, new_rows, wstate, wnew, pages, page_table, n_cmp, n_sel, emit_win):
    bsz, n_pages = page_table.shape
    win_buf = wstate.shape[1]
    n_seq = SEQ_PER_STEP if bsz % SEQ_PER_STEP == 0 else 1
    per_seq = lambda a: pl.BlockSpec((n_seq,) + a.shape[1:], lambda b, pt: (b,) + (0,) * (a.ndim - 1))
    out_specs = [pl.BlockSpec((n_seq, N_HEADS, KV_WIDTH), lambda b, pt: (b, 0, 0))]
    out_shape = [jax.ShapeDtypeStruct((bsz, N_HEADS, KV_WIDTH), bf16)]
    if emit_win:
        out_specs.append(per_seq(wstate))
        out_shape.append(jax.ShapeDtypeStruct(wstate.shape, wstate.dtype))
    return pl.pallas_call(
        functools.partial(_attn_sample_body, n_cmp, n_sel, win_buf, emit_win, n_pages),
        grid_spec=pltpu.PrefetchScalarGridSpec(
            num_scalar_prefetch=1,
            grid=(bsz // n_seq,),
            in_specs=[per_seq(q_raw), per_seq(q_rot), per_seq(gates), per_seq(kc), per_seq(vc),
                      pl.BlockSpec(c2s.shape, lambda b, pt: (0, 0)), per_seq(new_rows), per_seq(wstate), per_seq(wnew)]
            + _page_specs(n_seq, n_pages, 1),
            out_specs=out_specs),
        out_shape=out_shape,
        compiler_params=_cparams(("arbitrary",)),
        name="attn_sample",
    )(page_table.reshape(-1), q_raw, q_rot, gates, kc, vc, c2s, new_rows, wstate, wnew, *([pages] * (n_seq * n_pages)))


def _rope_tables(pos):
    half = ROT_DIM // 2
    inv = np.power(ROPE_THETA, -np.arange(half, dtype=np.float64) * (2.0 / ROT_DIM))
    ang = np.asarray(pos, np.float64)[:, None] * inv[None, :]
    cos_h = np.ones((len(pos), HEAD_DIM), np.float32)
    sin_h = np.zeros((len(pos), HEAD_DIM), np.float32)
    cos_h[:, :half] = np.cos(ang)
    cos_h[:, half:ROT_DIM] = np.cos(ang)
    sin_h[:, :half] = -np.sin(ang)
    sin_h[:, half:ROT_DIM] = np.sin(ang)
    return (jnp.asarray(np.tile(cos_h, (1, 2))), jnp.asarray(np.tile(sin_h, (1, 2))),
            jnp.asarray(cos_h.T.copy()), jnp.asarray(sin_h.T.copy()))


def _cmp_to_sel(t, n_rows, transposed):
    nc = (t - L_CMP) // D_CMP + 1
    ns = -(-t // L_SEL)
    c_start = np.arange(nc) * D_CMP
    s_start = np.arange(ns) * L_SEL
    m = np.zeros((n_rows, LANES), np.float32)
    m[:nc, :ns] = (c_start[:, None] < s_start[None, :] + L_SEL) & (c_start[:, None] + L_CMP > s_start[None, :])
    return jnp.asarray(m.T.copy() if transposed else m, bf16), nc, ns


def _gate_blockdiag(w_gate):
    wg = w_gate.astype(bf16).reshape(2, LRU_HEADS // 2, 2, LRU_BLOCK, LRU_BLOCK)
    z = jnp.zeros_like(wg[:, :, 0])
    top = jnp.concatenate([wg[:, :, 0], z], axis=-1)
    bot = jnp.concatenate([z, wg[:, :, 1]], axis=-1)
    return jnp.concatenate([top, bot], axis=-2)


def _compress_weights(pe_cmp, w_cmp1, b_cmp1, w_cmp2, per_group):
    pe = pe_cmp.reshape(2, 1, L_CMP * HEAD_DIM)
    w1 = w_cmp1.astype(bf16)
    w1p = w1.reshape(2, 2, D_CMP // 2, 2, HEAD_DIM, CMP_HIDDEN)
    eye2 = jnp.eye(2, dtype=bf16)
    wp = jnp.einsum('shpldk,ab->spladbhk', w1p, eye2).reshape(2, D_CMP // 2, 2 * LANES, 4 * CMP_HIDDEN)
    b1 = b_cmp1.reshape(2, 1, CMP_HIDDEN)
    w2 = w_cmp2.astype(bf16)
    eye = jnp.eye(N_KV_HEADS, dtype=bf16)
    if per_group:
        pad = jnp.zeros((CMP_HIDDEN, CMP_HIDDEN - HEAD_DIM), bf16)
        w2x = jnp.stack([jnp.concatenate([w2[0], pad], axis=1), jnp.concatenate([w2[1].T, pad.T], axis=0)])
    else:
        w2x = jnp.einsum('skd,gq->sgkqd', w2, eye).reshape(2, N_KV_HEADS, CMP_HIDDEN, KV_WIDTH)
    return pe, w1, wp, b1, w2x


def _q_weights_prompt(w_qg):
    wqt = w_qg[:, :Q_WIDTH].T.astype(bf16)
    wg = w_qg[:, Q_WIDTH:].T.reshape(N_KV_HEADS, 3 * GROUP, D_MODEL)
    wgt = jnp.pad(wg, ((0, 0), (0, 2 * SUBLANES - 3 * GROUP), (0, 0))).reshape(N_KV_HEADS * 2 * SUBLANES, D_MODEL).astype(bf16)
    return wqt, wgt


def _q_weights_sample(w_qg):
    wq = w_qg[:, :Q_WIDTH].astype(bf16).reshape(D_MODEL, N_KV_HEADS, GROUP, HEAD_DIM)
    eye = jnp.eye(N_KV_HEADS, dtype=bf16)
    wq = jnp.einsum('kgrd,gq->kgrqd', wq, eye).reshape(D_MODEL, N_HEADS * KV_WIDTH)
    wg = w_qg[:, Q_WIDTH:].reshape(D_MODEL, N_HEADS, 3)
    wg = jnp.pad(wg, ((0, 0), (0, 0), (0, LANES - 3))).reshape(D_MODEL, N_HEADS * LANES).astype(bf16)
    return wq, wg


def _out_weights_sample(w_o):
    w = w_o.astype(bf16).reshape(N_KV_HEADS, GROUP, HEAD_DIM, D_MODEL)
    eye = jnp.eye(N_KV_HEADS, dtype=bf16)
    return jnp.einsum('grdn,gq->grqdn', w, eye).reshape(N_HEADS * KV_WIDTH, D_MODEL)


def kernel(x_prompt, x_sample, c_prompt, c_sample, cache_nsa_kv, page_table, state_nsa_win, state_lru_h, state_lru_conv, state_ffn_conv, w_mod, b_mod, g_norm, w_lru_in, b_lru_in, w_lru_conv, b_lru_conv, w_lru_gate, b_lru_gate, lru_lambda, w_lru_out, g_kv, w_mod_kv, b_mod_kv, w_kv, pe_cmp, w_cmp1, b_cmp1, w_cmp2, w_nsa_qg, w_nsa_out, w_ffn_in, w_ffn_conv, b_ffn_conv, w_ffn_out):
    bp, tp, d = x_prompt.shape
    bs, ts, _ = x_sample.shape
    assert ts == 1 and d == D_MODEL
    depth = w_mod.shape[0]
    n_a = w_lru_in.shape[0]
    past_len = page_table.shape[1] * PAGE_SIZE
    win_buf = state_nsa_win.shape[1]

    c_all = jnp.concatenate([c_prompt, c_sample], axis=0)
    mod_all = _ada_mod(c_all, w_mod, b_mod).reshape(depth, bp + bs, N_MOD, d)
    mod_p = mod_all[:, :bp]
    mod_s = jnp.swapaxes(mod_all[:, bp:], 1, 2)
    modkv_all = _ada_mod(c_all, w_mod_kv[None], b_mod_kv[None]).reshape(bp + bs, 2, d)
    modkv_p = modkv_all[:bp]
    modkv_s = jnp.swapaxes(modkv_all[bp:], 0, 1)

    cos_p, sin_p, cos_pt, sin_pt = _rope_tables(np.arange(tp))
    cos_s, sin_s, _, _ = _rope_tables(np.asarray([past_len]))

    x_p = x_prompt
    x_s = x_sample.reshape(bs, d)
    lru_h_p, lru_h_s, lru_c_p, lru_c_s, ffn_c_p, ffn_c_s = [], [], [], [], [], []
    row1 = lambda a: a.reshape(1, -1)

    ctx_p = ctx_s = win_next_s = None
    keep_s = min(WINDOW, past_len + ts)
    for l in range(depth):
        g = g_norm[l]
        wi = w_ffn_in[l].astype(bf16)
        wo = w_ffn_out[l].astype(bf16)
        ffn_taps_s = jnp.swapaxes(state_ffn_conv[l], 0, 1)
        pre_p = pre_s = None
        if l < n_a:
            w_in = w_lru_in[l].astype(bf16)
            w_out = w_lru_out[l].astype(bf16)
            wg = _gate_blockdiag(w_lru_gate[l])
            args = (w_in, row1(b_lru_in[l]), w_lru_conv[l], row1(b_lru_conv[l]), wg, b_lru_gate[l], row1(lru_lambda[l]), w_out)
            x_p, h_last, cbuf = _lru_prompt(x_p, mod_p[l], g, *args)
            lru_h_p.append(h_last[:, 0])
            lru_c_p.append(cbuf[:, SUBLANES - (LRU_CONV_W - 1):])
            taps = jnp.swapaxes(state_lru_conv[l], 0, 1)
            x_s, h_new, up_new = _lru_sample(x_s, mod_s[l], g, state_lru_h[l], taps, *args)
            lru_h_s.append(h_new)
            lru_c_s.append(jnp.concatenate([state_lru_conv[l][:, 1:], up_new[:, None]], axis=1))
        else:
            j = l - n_a
            wq_p, wgt_p = _q_weights_prompt(w_nsa_qg[j])
            q_raw, q_rot, gates = _q_prompt(x_p, mod_p[l], g, wq_p, wgt_p, cos_pt, sin_pt)
            o_att = _attn_prompt(q_raw, q_rot, gates, *ctx_p)
            pre_p = (o_att, w_nsa_out[j].astype(bf16))
            wq_s, wgt_s = _q_weights_sample(w_nsa_qg[j])
            q_raw_s, q_rot_s, gates_s = _q_sample(x_s, mod_s[l], g, wq_s, wgt_s, cos_s, sin_s)
            emit_win = j == 0 and win_buf == keep_s
            res = _attn_sample(q_raw_s.reshape(bs, N_HEADS, KV_WIDTH), q_rot_s.reshape(bs, N_HEADS, KV_WIDTH),
                               gates_s.reshape(bs, N_HEADS, LANES), *ctx_s, emit_win)
            if emit_win:
                win_next_s = res[1]
            pre_s = (res[0].reshape(bs, N_HEADS * KV_WIDTH), _out_weights_sample(w_nsa_out[j]))
        x_p, fbuf = _ffn_prompt(x_p, mod_p[l], g, wi, w_ffn_conv[l], row1(b_ffn_conv[l]), wo, pre=pre_p)
        ffn_c_p.append(fbuf[:, SUBLANES - (FFN_CONV_W - 1):])
        x_s, z_new = _ffn_sample(x_s, mod_s[l], g, ffn_taps_s, wi, w_ffn_conv[l], row1(b_ffn_conv[l]), wo, pre=pre_s)
        ffn_c_s.append(jnp.concatenate([state_ffn_conv[l][:, 1:], z_new[:, None]], axis=1))

        if l == n_a - 1:
            w_kv_b = w_kv.astype(bf16)
            w_vt = jnp.concatenate([w_kv[:, 3 * KV_WIDTH:4 * KV_WIDTH].T, w_kv[:, 5 * KV_WIDTH:6 * KV_WIDTH].T], axis=0).astype(bf16)
            rows_p, win_p, kaug, kwin, vst, vwt = _kv_prompt(x_p, modkv_p, row1(g_kv), w_kv_b, w_vt, cos_p, sin_p)
            kc_p, vct_p = _compress(rows_p, _compress_weights(pe_cmp, w_cmp1, b_cmp1, w_cmp2, True), True)
            c2st_p, nc_p, ns_p = _cmp_to_sel(tp, kc_p.shape[2], True)
            ctx_p = (kc_p, vct_p, c2st_p, kaug, vst, kwin, vwt, nc_p, ns_p)
            rows_s, win_s = _kv_sample(x_s, modkv_s, row1(g_kv), w_kv_b, cos_s, sin_s)
            pages = cache_nsa_kv.reshape(cache_nsa_kv.shape[0], PAGE_SIZE, 4 * KV_WIDTH)
            kc_s, vc_s = _compress_paged(pages, page_table, _compress_weights(pe_cmp, w_cmp1, b_cmp1, w_cmp2, False))
            c2s_s, nc_s, ns_s = _cmp_to_sel(past_len + ts, kc_s.shape[1], False)
            ctx_s = (kc_s, vc_s, c2s_s, rows_s.reshape(bs, 1, 4 * KV_WIDTH), state_nsa_win.reshape(bs, win_buf, 2 * KV_WIDTH),
                     win_s.reshape(bs, 1, 2 * KV_WIDTH), pages, page_table, nc_s, ns_s)

    keep_p = min(WINDOW, tp)
    kv_rows_p = rows_p.reshape(bp, tp, 4, N_KV_HEADS, HEAD_DIM)
    win_out_p = win_p[:, tp - keep_p:].reshape(bp, keep_p, 2, N_KV_HEADS, HEAD_DIM)
    kv_rows_s = rows_s.reshape(bs, ts, 4, N_KV_HEADS, HEAD_DIM)
    if win_next_s is not None:
        win_out_s = win_next_s.reshape(bs, keep_s, 2, N_KV_HEADS, HEAD_DIM)
    else:
        win_all = jnp.concatenate([state_nsa_win, win_s.reshape(bs, ts, 2, N_KV_HEADS, HEAD_DIM)], axis=1)
        win_out_s = win_all[:, win_buf + ts - keep_s:]
    return (x_p, x_s.reshape(bs, ts, d), kv_rows_p, kv_rows_s, win_out_p, win_out_s,
            jnp.stack(lru_h_p), jnp.stack(lru_h_s), jnp.stack(lru_c_p), jnp.stack(lru_c_s),
            jnp.stack(ffn_c_p), jnp.stack(ffn_c_s))
```

```python
import functools

import numpy as np
import jax
import jax.numpy as jnp
from jax import lax
from jax.experimental import pallas as pl
from jax.experimental.pallas import tpu as pltpu

f32 = jnp.float32
bf16 = jnp.bfloat16

D_MODEL = 1024
D_RNN = D_MODEL
LRU_HEADS = 8
LRU_BLOCK = D_RNN // LRU_HEADS
LRU_CONV_W = 4
LRU_C = 8.0
N_HEADS = 16
HEAD_DIM = 64
N_KV_HEADS = 4
GROUP = N_HEADS // N_KV_HEADS
Q_WIDTH = N_HEADS * HEAD_DIM
KV_WIDTH = N_KV_HEADS * HEAD_DIM
L_CMP = 32
D_CMP = 16
CMP_HIDDEN = 128
L_SEL = 64
TOP_N = 8
WINDOW = 512
ROT_DIM = HEAD_DIM // 4
ROPE_THETA = 500000.0
D_FF = 3 * D_MODEL
FFN_CONV_W = 3
N_MOD = 6
EPS = 1e-6
NEG_INF = -1e30
FORCED_SCORE = 1e6
PAGE_SIZE = 128

LANES = 128
SUBLANES = 8
VMEM_LIMIT = 60 * 1024 * 1024
ROW_TILE = 256
FFN_TILE = 512
Q_TILE = 512
K_CHUNK = 512
FF_CHUNK = 1024
BELOW_ALL = -3.0e38
SEQ_PER_STEP = 2
SEL_GROUP = 16


def _cparams(sem=None):
    return pltpu.CompilerParams(dimension_semantics=sem, vmem_limit_bytes=VMEM_LIMIT)


def _const_spec(shape):
    nd = len(shape)
    return pl.BlockSpec(shape, lambda *_: (0,) * nd, pipeline_mode=pl.Buffered(1))


def _rms(x, g):
    return x * lax.rsqrt(jnp.mean(x * x, axis=-1, keepdims=True) + EPS) * g


def _gelu(x):
    return 0.5 * x * (1.0 + jnp.tanh(0.7978845608028654 * (x + 0.044715 * (x * x * x))))


def _mm(a, w):
    return jnp.dot(a.astype(bf16), w, preferred_element_type=f32)


def _mm_nt(a, b):
    return lax.dot_general(a, b, (((1,), (1,)), ((), ())), preferred_element_type=f32)


def _mm_exact_rhs(a, b01):
    a1 = a.astype(bf16)
    r1 = a - a1.astype(f32)
    a2 = r1.astype(bf16)
    a3 = (r1 - a2.astype(f32)).astype(bf16)
    return (jnp.dot(a1, b01, preferred_element_type=f32) + jnp.dot(a2, b01, preferred_element_type=f32)
            + jnp.dot(a3, b01, preferred_element_type=f32))


def _mm_exact_lhs(a01, b):
    b1 = b.astype(bf16)
    r1 = b - b1.astype(f32)
    b2 = r1.astype(bf16)
    b3 = (r1 - b2.astype(f32)).astype(bf16)
    return (jnp.dot(a01, b1, preferred_element_type=f32) + jnp.dot(a01, b2, preferred_element_type=f32)
            + jnp.dot(a01, b3, preferred_element_type=f32))


def _shift_rows(cur, tail, k):
    r = pltpu.roll(cur, k, 0)
    rp = pltpu.roll(tail, k, 0)
    row = lax.broadcasted_iota(jnp.int32, tail.shape, 0)
    first = jnp.where(row < k, rp, r[0:SUBLANES])
    return jnp.concatenate([first, r[SUBLANES:]], axis=0)


def _causal_conv_tile(cur, tail, w, b):
    width = w.shape[0]
    out = b + w[width - 1:width] * cur
    for k in range(1, width):
        out = out + w[width - 1 - k:width - k] * _shift_rows(cur, tail, k)
    return out


def _lru_gates(u, wg_ref, bg, lam):
    pair = 2 * LRU_BLOCK
    n_pair = D_RNN // pair
    ub = u.astype(bf16)
    gates = []
    for gi in range(2):
        cols = [jnp.dot(ub[:, p * pair:(p + 1) * pair], wg_ref[gi, p], preferred_element_type=f32) for p in range(n_pair)]
        gates.append(jnp.concatenate(cols, axis=1) + bg[gi:gi + 1])
    r = jax.nn.sigmoid(gates[0])
    i = jax.nn.sigmoid(gates[1])
    z = -lam
    softplus = jnp.maximum(z, 0.0) + jnp.log1p(jnp.exp(-jnp.abs(z)))
    log_a = -LRU_C * r * softplus
    a = jnp.exp(log_a)
    mult = jnp.sqrt(-jnp.tanh(log_a) * (a * a + 1.0))
    return a, mult * (i * u)


def _scan_rows(a, b, h0):
    n, c = a.shape
    n_grp = n // SUBLANES
    a = a.reshape(n_grp, SUBLANES, c)
    b = b.reshape(n_grp, SUBLANES, c)
    row = lax.broadcasted_iota(jnp.int32, a.shape, 1)
    s = 1
    while s < SUBLANES:
        a_sh = jnp.where(row >= s, pltpu.roll(a, s, 1), 1.0)
        b_sh = jnp.where(row >= s, pltpu.roll(b, s, 1), 0.0)
        b = a * b_sh + b
        a = a * a_sh
        s *= 2
    out = []
    carry = h0
    for i in range(n_grp):
        h = b[i] + a[i] * carry
        out.append(h)
        carry = h[SUBLANES - 1:SUBLANES]
    return jnp.concatenate(out, axis=0)


def _rope_cols(x, cos_t, sin_t):
    lane = lax.broadcasted_iota(jnp.int32, x.shape, 1)
    half = ROT_DIM // 2
    swapped = jnp.where(lane % HEAD_DIM < half, pltpu.roll(x, LANES - half, 1), pltpu.roll(x, half, 1))
    return x * cos_t + swapped * sin_t


def _rope_wide(x, cos_t, sin_t):
    n = x.shape[1] // LANES
    return jnp.concatenate([_rope_cols(x[:, j * LANES:(j + 1) * LANES], cos_t, sin_t) for j in range(n)], axis=1)


def _topk_mask(imp, cur, n_blocks):
    blk = lax.broadcasted_iota(jnp.int32, imp.shape, 1)
    forced = (blk == 0) | (blk == cur) | (blk == cur - 1)
    imp = jnp.where(forced, FORCED_SCORE, imp)
    imp = jnp.where(blk <= cur, imp, NEG_INF)
    v = jnp.where(blk < n_blocks, imp, BELOW_ALL)
    rank = jnp.zeros(imp.shape, jnp.int32)
    for i in range(n_blocks):
        vi = v[:, i:i + 1]
        rank = rank + ((vi > v) | ((vi == v) & (blk > i))).astype(jnp.int32)
    return jnp.where((rank < TOP_N) & (v > 0.5 * NEG_INF), 1.0, 0.0)


def _mod_body(c_ref, w_ref, b_ref, o_ref):
    c = c_ref[...]
    o_ref[0] = _mm(c * jax.nn.sigmoid(c), w_ref[0].astype(bf16)) + b_ref[0]


def _ada_mod(c, w, b):
    n_l, d, n = w.shape
    m = c.shape[0]
    tn = n // 2 if n % (2 * LANES) == 0 else n
    return pl.pallas_call(
        _mod_body,
        grid=(n_l, n // tn),
        in_specs=[pl.BlockSpec((m, d), lambda l, j: (0, 0)),
                  pl.BlockSpec((1, d, tn), lambda l, j: (l, 0, j)),
                  pl.BlockSpec((1, 1, tn), lambda l, j: (l, 0, j))],
        out_specs=pl.BlockSpec((1, m, tn), lambda l, j: (l, 0, j)),
        out_shape=jax.ShapeDtypeStruct((n_l, m, n), f32),
        compiler_params=_cparams(("arbitrary", "arbitrary")),
        name="ada_mod",
    )(c, w, b.reshape(n_l, 1, n))


def _lru_prompt_body(x_ref, mod_ref, g_ref, win_ref, bin_ref, wc_ref, bc_ref, wg_ref, bg_ref, lam_ref, wout_ref,
                     xo_ref, hlast_ref, cbuf_ref, tail_ref, h_ref):
    t = pl.program_id(1)

    @pl.when(t == 0)
    def _():
        tail_ref[...] = jnp.zeros(tail_ref.shape, f32)
        h_ref[...] = jnp.zeros(h_ref.shape, f32)

    x = x_ref[0]
    tm = x.shape[0]
    mod = mod_ref[0]
    g = g_ref[...]
    h = _rms(x, g[0:1]) * (1.0 + mod[1:2]) + mod[0:1]
    proj = _mm(h, win_ref[...]) + bin_ref[...]
    y = _gelu(proj[:, :D_RNN])
    up = proj[:, D_RNN:]
    u = _causal_conv_tile(up, tail_ref[...], wc_ref[...], bc_ref[...])
    last_rows = up[tm - SUBLANES:tm]
    tail_ref[...] = last_rows
    cbuf_ref[0] = last_rows
    a, bx = _lru_gates(u, wg_ref, bg_ref[...], lam_ref[...])
    hs = _scan_rows(a, bx, h_ref[0:1])
    h_fin = hs[tm - 1:tm]
    h_ref[...] = jnp.broadcast_to(h_fin, h_ref.shape)
    hlast_ref[0] = jnp.broadcast_to(h_fin, hlast_ref.shape[1:])
    out = _mm(hs * y, wout_ref[...])
    xo_ref[0] = x + mod[2:3] * _rms(out, g[1:2])


def _lru_prompt(x, mod, g, w_in, b_in, w_conv, b_conv, wg, bg, lam, w_out):
    bsz, t, d = x.shape
    tm = min(ROW_TILE, t)
    row_spec = pl.BlockSpec((1, tm, d), lambda b, i: (b, i, 0))
    small = lambda r, c: pl.BlockSpec((1, r, c), lambda b, i: (b, 0, 0))
    return pl.pallas_call(
        _lru_prompt_body,
        grid=(bsz, t // tm),
        in_specs=[row_spec, small(N_MOD, d), _const_spec(g.shape), _const_spec(w_in.shape), _const_spec(b_in.shape),
                  _const_spec(w_conv.shape), _const_spec(b_conv.shape), _const_spec(wg.shape), _const_spec(bg.shape),
                  _const_spec(lam.shape), _const_spec(w_out.shape)],
        out_specs=[row_spec, small(SUBLANES, D_RNN), small(SUBLANES, D_RNN)],
        out_shape=[jax.ShapeDtypeStruct((bsz, t, d), f32), jax.ShapeDtypeStruct((bsz, SUBLANES, D_RNN), f32),
                   jax.ShapeDtypeStruct((bsz, SUBLANES, D_RNN), f32)],
        scratch_shapes=[pltpu.VMEM((SUBLANES, D_RNN), f32), pltpu.VMEM((SUBLANES, D_RNN), f32)],
        compiler_params=_cparams(("arbitrary", "arbitrary")),
        name="lru_prompt",
    )(x, mod, g, w_in, b_in, w_conv, b_conv, wg, bg, lam, w_out)


def _lru_sample_body(x_ref, mod_ref, g_ref, h0_ref, taps_ref, win_ref, bin_ref, wc_ref, bc_ref, wg_ref, bg_ref, lam_ref,
                     wout_ref, xo_ref, hnew_ref, up_ref):
    x = x_ref[...]
    g = g_ref[...]
    h = _rms(x, g[0:1]) * (1.0 + mod_ref[1]) + mod_ref[0]
    proj = _mm(h, win_ref[...]) + bin_ref[...]
    y = _gelu(proj[:, :D_RNN])
    up = proj[:, D_RNN:]
    up_ref[...] = up
    wc = wc_ref[...]
    u = bc_ref[...] + wc[LRU_CONV_W - 1:LRU_CONV_W] * up
    for k in range(LRU_CONV_W - 1):
        u = u + wc[k:k + 1] * taps_ref[k]
    a, bx = _lru_gates(u, wg_ref, bg_ref[...], lam_ref[...])
    hs = a * h0_ref[...] + bx
    hnew_ref[...] = hs
    out = _mm(hs * y, wout_ref[...])
    xo_ref[...] = x + mod_ref[2] * _rms(out, g[1:2])


def _lru_sample(x, mod, g, h0, taps, w_in, b_in, w_conv, b_conv, wg, bg, lam, w_out):
    m, d = x.shape
    return pl.pallas_call(
        _lru_sample_body,
        out_shape=[jax.ShapeDtypeStruct((m, d), f32), jax.ShapeDtypeStruct((m, D_RNN), f32),
                   jax.ShapeDtypeStruct((m, D_RNN), f32)],
        compiler_params=_cparams(),
        name="lru_sample",
    )(x, mod, g, h0, taps, w_in, b_in, w_conv, b_conv, wg, bg, lam, w_out)


def _ffn_core(h, tap_fn, wi_ref, wc, bc, wo_ref, store_fn):
    acc = jnp.zeros((h.shape[0], D_MODEL), f32)
    hb = h.astype(bf16)
    for c in range(D_FF // FF_CHUNK):
        zs = []
        for base in (0, D_FF):
            lo = base + c * FF_CHUNK
            z = jnp.dot(hb, wi_ref[:, lo:lo + FF_CHUNK], preferred_element_type=f32)
            store_fn(z, lo)
            zc = bc[:, lo:lo + FF_CHUNK] + wc[FFN_CONV_W - 1:FFN_CONV_W, lo:lo + FF_CHUNK] * z
            for k in range(1, FFN_CONV_W):
                zc = zc + wc[FFN_CONV_W - 1 - k:FFN_CONV_W - k, lo:lo + FF_CHUNK] * tap_fn(z, lo, k)
            zs.append(zc)
        gz = _gelu(zs[0]) * zs[1]
        acc = acc + jnp.dot(gz.astype(bf16), wo_ref[c * FF_CHUNK:(c + 1) * FF_CHUNK, :], preferred_element_type=f32)
    return acc


def _ffn_prompt_body(has_pre, *refs):
    if has_pre:
        (x_ref, o_ref, wpre_ref, mod_ref, g_ref, wi_ref, wc_ref, bc_ref, wo_ref, xo_ref, fbuf_ref, tail_ref) = refs
    else:
        (x_ref, mod_ref, g_ref, wi_ref, wc_ref, bc_ref, wo_ref, xo_ref, fbuf_ref, tail_ref) = refs
    t = pl.program_id(1)

    @pl.when(t == 0)
    def _():
        tail_ref[...] = jnp.zeros(tail_ref.shape, f32)

    x = x_ref[0]
    tm = x.shape[0]
    mod = mod_ref[0]
    g = g_ref[...]
    if has_pre:
        x = x + mod[2:3] * _rms(jnp.dot(o_ref[0], wpre_ref[...], preferred_element_type=f32), g[1:2])
    h = _rms(x, g[2:3]) * (1.0 + mod[4:5]) + mod[3:4]

    def tap(z, lo, k):
        return _shift_rows(z, tail_ref[:, lo:lo + FF_CHUNK], k)

    def store(z, lo):
        fbuf_ref[0, :, lo:lo + FF_CHUNK] = z[tm - SUBLANES:tm]

    acc = _ffn_core(h, tap, wi_ref, wc_ref[...], bc_ref[...], wo_ref, store)
    tail_ref[...] = fbuf_ref[0]
    xo_ref[0] = x + mod[5:6] * _rms(acc, g[3:4])


def _ffn_prompt(x, mod, g, w_in, w_conv, b_conv, w_out, pre=None):
    bsz, t, d = x.shape
    tm = min(FFN_TILE, t)
    row_spec = pl.BlockSpec((1, tm, d), lambda b, i: (b, i, 0))
    small = lambda r, c: pl.BlockSpec((1, r, c), lambda b, i: (b, 0, 0))
    ins, specs = [x], [row_spec]
    if pre is not None:
        o_att, w_pre = pre
        ins += [o_att, w_pre]
        specs += [pl.BlockSpec((1, tm, o_att.shape[2]), lambda b, i: (b, i, 0)), _const_spec(w_pre.shape)]
    ins += [mod, g, w_in, w_conv, b_conv, w_out]
    specs += [small(N_MOD, d), _const_spec(g.shape), _const_spec(w_in.shape), _const_spec(w_conv.shape),
              _const_spec(b_conv.shape), _const_spec(w_out.shape)]
    return pl.pallas_call(
        functools.partial(_ffn_prompt_body, pre is not None),
        grid=(bsz, t // tm),
        in_specs=specs,
        out_specs=[row_spec, small(SUBLANES, 2 * D_FF)],
        out_shape=[jax.ShapeDtypeStruct((bsz, t, d), f32), jax.ShapeDtypeStruct((bsz, SUBLANES, 2 * D_FF), f32)],
        scratch_shapes=[pltpu.VMEM((SUBLANES, 2 * D_FF), f32)],
        compiler_params=_cparams(("arbitrary", "arbitrary")),
        name="ffn_prompt",
    )(*ins)


def _ffn_sample_body(has_pre, *refs):
    if has_pre:
        (x_ref, o_ref, wpre_ref, mod_ref, g_ref, taps_ref, wi_ref, wc_ref, bc_ref, wo_ref, xo_ref, z_ref) = refs
    else:
        (x_ref, mod_ref, g_ref, taps_ref, wi_ref, wc_ref, bc_ref, wo_ref, xo_ref, z_ref) = refs
    x = x_ref[...]
    g = g_ref[...]
    if has_pre:
        x = x + mod_ref[2] * _rms(jnp.dot(o_ref[...], wpre_ref[...], preferred_element_type=f32), g[1:2])
    h = _rms(x, g[2:3]) * (1.0 + mod_ref[4]) + mod_ref[3]

    def tap(z, lo, k):
        return taps_ref[FFN_CONV_W - 1 - k, :, lo:lo + FF_CHUNK]

    def store(z, lo):
        z_ref[:, lo:lo + FF_CHUNK] = z

    acc = _ffn_core(h, tap, wi_ref, wc_ref[...], bc_ref[...], wo_ref, store)
    xo_ref[...] = x + mod_ref[5] * _rms(acc, g[3:4])


def _ffn_sample(x, mod, g, taps, w_in, w_conv, b_conv, w_out, pre=None):
    m, d = x.shape
    ins = [x] + (list(pre) if pre is not None else []) + [mod, g, taps, w_in, w_conv, b_conv, w_out]
    return pl.pallas_call(
        functools.partial(_ffn_sample_body, pre is not None),
        out_shape=[jax.ShapeDtypeStruct((m, d), f32), jax.ShapeDtypeStruct((m, 2 * D_FF), f32)],
        compiler_params=_cparams(),
        name="ffn_sample",
    )(*ins)


def _head_cols(src, extra):
    lane = lax.broadcasted_iota(jnp.int32, (src.shape[0], LANES), 1)
    out = []
    for j in range(KV_WIDTH // LANES):
        col = src[:, j * LANES:(j + 1) * LANES]
        out.append(jnp.where(lane < HEAD_DIM, col, extra))
        out.append(jnp.where(lane < HEAD_DIM, pltpu.roll(col, HEAD_DIM, 1), extra))
    return out


def _kv_prompt_body(x_ref, mod_ref, g_ref, w_ref, wvt_ref, cos_ref, sin_ref, rows_ref, win_ref, kaug_ref, kwin_ref, vst_ref, vwt_ref):
    x = x_ref[0]
    tm = x.shape[0]
    mod = mod_ref[0]
    h = _rms(x, g_ref[...]) * (1.0 + mod[1:2]) + mod[0:1]
    hb = h.astype(bf16)
    kv = jnp.dot(hb, w_ref[...], preferred_element_type=f32)
    vt = _mm_nt(wvt_ref[...], hb)
    cos_t, sin_t = cos_ref[...], sin_ref[...]
    w = KV_WIDTH
    k_sel = _rope_wide(kv[:, 2 * w:3 * w], cos_t, sin_t)
    k_win = _rope_wide(kv[:, 4 * w:5 * w], cos_t, sin_t)
    rows_ref[0, :, 0:2 * w] = kv[:, 0:2 * w]
    rows_ref[0, :, 2 * w:3 * w] = k_sel
    rows_ref[0, :, 3 * w:4 * w] = kv[:, 3 * w:4 * w]
    win_ref[0, :, 0:w] = k_win
    win_ref[0, :, w:2 * w] = kv[:, 5 * w:6 * w]
    row = lax.broadcasted_iota(jnp.int32, (tm, LANES), 0)
    lane = lax.broadcasted_iota(jnp.int32, (tm, LANES), 1)
    pos = pl.program_id(1) * tm + row
    onehot = jnp.where(lane - HEAD_DIM == (pos // L_SEL) % SEL_GROUP, 1.0, 0.0)
    for gi, col in enumerate(_head_cols(k_sel, onehot)):
        for c in range(tm // K_CHUNK):
            kaug_ref[0, gi, c] = col[c * K_CHUNK:(c + 1) * K_CHUNK].astype(bf16)
    for gi, col in enumerate(_head_cols(k_win, 0.0)):
        for c in range(tm // Q_TILE):
            kwin_ref[0, gi, c] = col[c * Q_TILE:(c + 1) * Q_TILE, 0:HEAD_DIM].astype(bf16)
    for gi in range(N_KV_HEADS):
        v_s = vt[gi * HEAD_DIM:(gi + 1) * HEAD_DIM].astype(bf16)
        v_w = vt[w + gi * HEAD_DIM:w + (gi + 1) * HEAD_DIM].astype(bf16)
        for c in range(tm // K_CHUNK):
            vst_ref[0, gi, c] = v_s[:, c * K_CHUNK:(c + 1) * K_CHUNK]
        for c in range(tm // Q_TILE):
            vwt_ref[0, gi, c] = v_w[:, c * Q_TILE:(c + 1) * Q_TILE]


def _kv_prompt(x, mod, g_kv, w_kv, w_vt, cos_t, sin_t):
    bsz, t, d = x.shape
    tm = K_CHUNK
    nc, nt = tm // K_CHUNK, tm // Q_TILE
    row = lambda c: pl.BlockSpec((1, tm, c), lambda b, i: (b, i, 0))
    chunked = lambda n, r, c: pl.BlockSpec((1, N_KV_HEADS, n, r, c), lambda b, i: (b, 0, i, 0, 0))
    shape5 = lambda n, r, c: jax.ShapeDtypeStruct((bsz, N_KV_HEADS, n, r, c), bf16)
    return pl.pallas_call(
        _kv_prompt_body,
        grid=(bsz, t // tm),
        in_specs=[row(d), pl.BlockSpec((1, 2, d), lambda b, i: (b, 0, 0)), _const_spec(g_kv.shape), _const_spec(w_kv.shape),
                  _const_spec(w_vt.shape), pl.BlockSpec((tm, LANES), lambda b, i: (i, 0)),
                  pl.BlockSpec((tm, LANES), lambda b, i: (i, 0))],
        out_specs=[row(4 * KV_WIDTH), row(2 * KV_WIDTH), chunked(nc, K_CHUNK, LANES), chunked(nt, Q_TILE, HEAD_DIM),
                   chunked(nc, HEAD_DIM, K_CHUNK), chunked(nt, HEAD_DIM, Q_TILE)],
        out_shape=[jax.ShapeDtypeStruct((bsz, t, 4 * KV_WIDTH), f32), jax.ShapeDtypeStruct((bsz, t, 2 * KV_WIDTH), f32),
                   shape5(t // K_CHUNK, K_CHUNK, LANES), shape5(t // Q_TILE, Q_TILE, HEAD_DIM),
                   shape5(t // K_CHUNK, HEAD_DIM, K_CHUNK), shape5(t // Q_TILE, HEAD_DIM, Q_TILE)],
        compiler_params=_cparams(("arbitrary", "arbitrary")),
        name="kv_prompt",
    )(x, mod, g_kv, w_kv, w_vt, cos_t, sin_t)


def _kv_sample_body(x_ref, mod_ref, g_ref, w_ref, cos_ref, sin_ref, rows_ref, win_ref):
    h = _rms(x_ref[...], g_ref[...]) * (1.0 + mod_ref[1]) + mod_ref[0]
    kv = _mm(h, w_ref[...])
    cos_t, sin_t = cos_ref[...], sin_ref[...]
    w = KV_WIDTH
    rows_ref[:, 0:2 * w] = kv[:, 0:2 * w]
    rows_ref[:, 2 * w:3 * w] = _rope_wide(kv[:, 2 * w:3 * w], cos_t, sin_t)
    rows_ref[:, 3 * w:4 * w] = kv[:, 3 * w:4 * w]
    win_ref[:, 0:w] = _rope_wide(kv[:, 4 * w:5 * w], cos_t, sin_t)
    win_ref[:, w:2 * w] = kv[:, 5 * w:6 * w]


def _kv_sample(x, mod, g_kv, w_kv, cos_t, sin_t):
    m = x.shape[0]
    return pl.pallas_call(
        _kv_sample_body,
        out_shape=[jax.ShapeDtypeStruct((m, 4 * KV_WIDTH), f32), jax.ShapeDtypeStruct((m, 2 * KV_WIDTH), f32)],
        compiler_params=_cparams(),
        name="kv_sample",
    )(x, mod, g_kv, w_kv, cos_t, sin_t)


def _compress_body(per_group, *refs):
    n_slab = 2 * KV_WIDTH // LANES
    slabs = refs[:n_slab]
    pe_ref, w1_ref, wp_ref, b1_ref, w2_ref = refs[n_slab:n_slab + 5]
    outs = refs[n_slab + 5:]
    n_chunk = slabs[0].shape[1] // D_CMP
    rows_of = lambda j, start: slabs[j][0, pl.ds(start, n_chunk, stride=D_CMP), :]
    per_slot = KV_WIDTH // LANES
    hid = CMP_HIDDEN
    for s in range(2):
        const = _mm(jnp.broadcast_to(pe_ref[s], (SUBLANES, L_CMP * HEAD_DIM)), w1_ref[s])[0:1] + b1_ref[s]
        tok = jnp.zeros((n_chunk, KV_WIDTH), f32)
        for j in range(per_slot):
            acc = jnp.zeros((n_chunk, 4 * hid), f32)
            for rp in range(D_CMP // 2):
                lhs = jnp.concatenate([rows_of(s * per_slot + j, 2 * rp), rows_of(s * per_slot + j, 2 * rp + 1)], axis=1)
                acc = acc + _mm(lhs, wp_ref[s, rp])
            for hl in range(2):
                gi = 2 * j + hl
                first = acc[:, 2 * hl * hid:(2 * hl + 1) * hid]
                second = acc[:, (2 * hl + 1) * hid:(2 * hl + 2) * hid]
                act_g = _gelu(first + pltpu.roll(second, n_chunk - 1, 0) + const).astype(bf16)
                if not per_group:
                    tok = tok + jnp.dot(act_g, w2_ref[s, gi], preferred_element_type=f32)
                elif s == 0:
                    outs[s][0, gi] = jnp.dot(act_g, w2_ref[s], preferred_element_type=f32)[:, 0:HEAD_DIM].astype(bf16)
                else:
                    outs[s][0, gi] = _mm_nt(w2_ref[s], act_g)[0:HEAD_DIM].astype(bf16)
        if not per_group:
            outs[s][0] = tok.astype(bf16)


def _compress(rows, cw, per_group):
    bsz, t = rows.shape[:2]
    n_chunk = t // D_CMP
    n_slab = 2 * KV_WIDTH // LANES
    slab_specs = [pl.BlockSpec((1, t, LANES), functools.partial(lambda j, b: (b, 0, j), j)) for j in range(n_slab)]
    pe, wt, wb, b1, w2 = cw
    if per_group:
        dims = [(n_chunk, HEAD_DIM), (HEAD_DIM, n_chunk)]
        out_specs = [pl.BlockSpec((1, N_KV_HEADS) + dm, lambda b: (b, 0, 0, 0)) for dm in dims]
        out_shapes = [jax.ShapeDtypeStruct((bsz, N_KV_HEADS) + dm, bf16) for dm in dims]
    else:
        out_specs = [pl.BlockSpec((1, n_chunk, KV_WIDTH), lambda b: (b, 0, 0))] * 2
        out_shapes = [jax.ShapeDtypeStruct((bsz, n_chunk, KV_WIDTH), bf16)] * 2
    return pl.pallas_call(
        functools.partial(_compress_body, per_group),
        grid=(bsz,),
        in_specs=slab_specs + [_const_spec(pe.shape), _const_spec(wt.shape), _const_spec(wb.shape), _const_spec(b1.shape),
                               _const_spec(w2.shape)],
        out_specs=out_specs,
        out_shape=out_shapes,
        compiler_params=_cparams(("arbitrary",)),
        name="compress_prompt" if per_group else "compress_sample",
    )(*([rows] * n_slab), pe, wt, wb, b1, w2)


def _q_prompt_body(x_ref, mod_ref, g_ref, wqt_ref, wgt_ref, cos_ref, sin_ref, qraw_ref, qrot_ref, gate_ref):
    mod = mod_ref[0]
    h = _rms(x_ref[0], g_ref[0:1]) * (1.0 + mod[1:2]) + mod[0:1]
    hb = h.astype(bf16)
    qt = _mm_nt(wqt_ref[...], hb)
    gate_ref[0] = jax.nn.sigmoid(_mm_nt(wgt_ref[...], hb))
    scale = HEAD_DIM ** -0.5
    qraw_ref[0] = (qt * scale).astype(bf16)
    half = ROT_DIM // 2
    dim = lax.broadcasted_iota(jnp.int32, qt.shape, 0) % HEAD_DIM
    swapped = jnp.where(dim < half, pltpu.roll(qt, Q_WIDTH - half, 0), pltpu.roll(qt, half, 0))
    cos_t = jnp.concatenate([cos_ref[...]] * N_HEADS, axis=0)
    sin_t = jnp.concatenate([sin_ref[...]] * N_HEADS, axis=0)
    qrot_ref[0] = ((qt * cos_t + swapped * sin_t) * scale).astype(bf16)


def _q_prompt(x, mod, g, wqt, wgt, cos_tt, sin_tt):
    bsz, t, d = x.shape
    tm = min(ROW_TILE, t)
    col = lambda r: pl.BlockSpec((1, r, tm), lambda b, i: (b, 0, i))
    tab = pl.BlockSpec((HEAD_DIM, tm), lambda b, i: (0, i))
    return pl.pallas_call(
        _q_prompt_body,
        grid=(bsz, t // tm),
        in_specs=[pl.BlockSpec((1, tm, d), lambda b, i: (b, i, 0)), pl.BlockSpec((1, N_MOD, d), lambda b, i: (b, 0, 0)),
                  _const_spec(g.shape), _const_spec(wqt.shape), _const_spec(wgt.shape), tab, tab],
        out_specs=[col(Q_WIDTH), col(Q_WIDTH), col(wgt.shape[0])],
        out_shape=[jax.ShapeDtypeStruct((bsz, Q_WIDTH, t), bf16), jax.ShapeDtypeStruct((bsz, Q_WIDTH, t), bf16),
                   jax.ShapeDtypeStruct((bsz, wgt.shape[0], t), f32)],
        compiler_params=_cparams(("arbitrary", "arbitrary")),
        name="q_prompt",
    )(x, mod, g, wqt, wgt, cos_tt, sin_tt)


def _q_sample_body(x_ref, mod_ref, g_ref, wq_ref, wgt_ref, cos_ref, sin_ref, qraw_ref, qrot_ref, gate_ref):
    h = _rms(x_ref[...], g_ref[0:1]) * (1.0 + mod_ref[1]) + mod_ref[0]
    hb = h.astype(bf16)
    q = jnp.dot(hb, wq_ref[...], preferred_element_type=f32)
    gate_ref[...] = jax.nn.sigmoid(jnp.dot(hb, wgt_ref[...], preferred_element_type=f32))
    scale = HEAD_DIM ** -0.5
    qraw_ref[...] = (q * scale).astype(bf16)
    qrot_ref[...] = (_rope_wide(q, cos_ref[...], sin_ref[...]) * scale).astype(bf16)


def _q_sample(x, mod, g, wq, wgt, cos_t, sin_t):
    m = x.shape[0]
    return pl.pallas_call(
        _q_sample_body,
        out_shape=[jax.ShapeDtypeStruct((m, wq.shape[1]), bf16), jax.ShapeDtypeStruct((m, wq.shape[1]), bf16),
                   jax.ShapeDtypeStruct((m, wgt.shape[1]), f32)],
        compiler_params=_cparams(),
        name="q_sample",
    )(x, mod, g, wq, wgt, cos_t, sin_t)


def _rank_select(imp, cur, n_blocks):
    blk = lax.broadcasted_iota(jnp.int32, imp.shape, 0)
    forced = (blk == 0) | (blk == cur) | (blk == cur - 1)
    v = jnp.where(forced, FORCED_SCORE, imp)
    v = jnp.where(blk <= cur, v, NEG_INF)
    v = jnp.where(blk < n_blocks, v, BELOW_ALL)
    rank = jnp.zeros(imp.shape, jnp.int32)
    for i in range(n_blocks):
        vi = v[i:i + 1, :]
        rank = rank + ((vi > v) | ((vi == v) & (blk > i))).astype(jnp.int32)
    return (rank < TOP_N) & (v > 0.5 * NEG_INF)


def _attn_prompt_body(n_cmp, n_sel, qraw_ref, qrot_ref, gate_ref, kc_ref, vct_ref, c2st_ref, kaug_ref, vst_ref, kwin_ref,
                      vwt_ref, o_ref, qaug_ref, negsel_ref):
    qi = pl.program_id(2)
    tq = qraw_ref.shape[2]
    n4 = GROUP * tq
    q_chunk = (qi * tq) // K_CHUNK

    def stack(ref):
        return jnp.concatenate([ref[0, r * HEAD_DIM:(r + 1) * HEAD_DIM, :] for r in range(GROUP)], axis=1)

    q_raw = stack(qraw_ref)
    q_rot = stack(qrot_ref)
    q_pos = qi * tq + lax.broadcasted_iota(jnp.int32, (1, n4), 1) % tq

    s = jnp.dot(kc_ref[0, 0], q_raw, preferred_element_type=f32)
    blk = lax.broadcasted_iota(jnp.int32, s.shape, 0)
    mask = (blk * D_CMP + (L_CMP - 1) <= q_pos) & (blk < n_cmp)
    s = jnp.where(mask, s, NEG_INF)
    p = jnp.where(mask, jnp.exp(s - jnp.max(s, axis=0, keepdims=True)), 0.0)
    l = jnp.sum(p, axis=0, keepdims=True)
    p = p / jnp.where(l > 0.0, l, 1.0)
    o_c = jnp.dot(vct_ref[0, 0], p.astype(bf16), preferred_element_type=f32)

    p_group = p[:, 0:tq]
    for r in range(1, GROUP):
        p_group = p_group + p[:, r * tq:(r + 1) * tq]
    imp = _mm_exact_lhs(c2st_ref[...], p_group)
    n_rows = negsel_ref.shape[0]
    picked = _rank_select(imp[0:n_rows], q_pos[:, 0:tq] // L_SEL, n_sel)
    negsel = jnp.where(picked, 0.0, NEG_INF)
    negsel_ref[...] = jnp.concatenate([negsel] * GROUP, axis=1)

    qaug_ref[0:HEAD_DIM, :] = q_rot
    qaug_ref[HEAD_DIM + SEL_GROUP:, :] = jnp.zeros((LANES - HEAD_DIM - SEL_GROUP, n4), bf16)
    blocks_per_chunk = K_CHUNK // L_SEL

    def chunk_scores(c):
        first = pl.multiple_of((c * blocks_per_chunk // SEL_GROUP) * SEL_GROUP, SEL_GROUP)
        qaug_ref[HEAD_DIM:HEAD_DIM + SEL_GROUP, :] = negsel_ref[pl.ds(first, SEL_GROUP), :].astype(bf16)
        return jnp.dot(kaug_ref[0, 0, c], qaug_ref[...], preferred_element_type=f32)

    def update(sc, carry, vt):
        m_old, l_old, acc_old = carry
        m_new = jnp.maximum(m_old, jnp.max(sc, axis=0, keepdims=True))
        alpha = jnp.exp(m_old - m_new)
        pc = jnp.exp(sc - m_new)
        l_new = alpha * l_old + jnp.sum(pc, axis=0, keepdims=True)
        acc_new = alpha * acc_old + jnp.dot(vt, pc.astype(bf16), preferred_element_type=f32)
        return m_new, l_new, acc_new

    init = (jnp.full((1, n4), NEG_INF, f32), jnp.zeros((1, n4), f32), jnp.zeros((HEAD_DIM, n4), f32))
    carry = lax.fori_loop(0, q_chunk, lambda c, cr: update(chunk_scores(c), cr, vst_ref[0, 0, c]), init)
    sc = chunk_scores(q_chunk)
    key_pos = q_chunk * K_CHUNK + lax.broadcasted_iota(jnp.int32, sc.shape, 0)
    sc = jnp.where(key_pos <= q_pos, sc, NEG_INF)
    _, l_s, acc_s = update(sc, carry, vst_ref[0, 0, q_chunk])

    n_back = WINDOW // tq
    row = lax.broadcasted_iota(jnp.int32, (tq, n4), 0)
    col = lax.broadcasted_iota(jnp.int32, (tq, n4), 1) % tq
    tiles, vts = [], []
    for j in range(n_back + 1):
        ti = qi - n_back + j
        tic = jnp.maximum(ti, 0)
        sw = jnp.dot(kwin_ref[0, 0, tic], q_rot, preferred_element_type=f32)
        if j < n_back:
            sw = sw + jnp.where(ti >= 0, 0.0, NEG_INF)
        if j == 0:
            sw = jnp.where(row > col, sw, NEG_INF)
        elif j == n_back:
            sw = jnp.where(row <= col, sw, NEG_INF)
        tiles.append(sw)
        vts.append(vwt_ref[0, 0, tic])
    sw = jnp.concatenate(tiles, axis=0)
    pw = jnp.exp(sw - jnp.max(sw, axis=0, keepdims=True))
    l_w = jnp.sum(pw, axis=0, keepdims=True)
    acc_w = jnp.dot(jnp.concatenate(vts, axis=1), pw.astype(bf16), preferred_element_type=f32)

    gate = gate_ref[0]
    o_s = acc_s / l_s
    o_w = acc_w / l_w
    heads = []
    for r in range(GROUP):
        sl = slice(r * tq, (r + 1) * tq)
        heads.append(gate[3 * r:3 * r + 1] * o_c[:, sl] + gate[3 * r + 1:3 * r + 2] * o_s[:, sl]
                     + gate[3 * r + 2:3 * r + 3] * o_w[:, sl])
    o_ref[0] = jnp.concatenate(heads, axis=0).T.astype(o_ref.dtype)


def _attn_prompt(qt_raw, qt_rot, gates_t, kc, vct, c2st, kaug, vst, kwin, vwt, n_cmp, n_sel):
    bsz, _, t = qt_raw.shape
    tq = Q_TILE
    gw = GROUP * HEAD_DIM
    n4 = GROUP * tq
    q_spec = pl.BlockSpec((1, gw, tq), lambda b, g, i: (b, g, i))
    gate_rows = gates_t.shape[1] // N_KV_HEADS
    per_group = lambda a: pl.BlockSpec((1, 1) + a.shape[2:], lambda b, g, i: (b, g) + (0,) * (a.ndim - 2))
    sel_rows = -(-n_sel // SEL_GROUP) * SEL_GROUP
    return pl.pallas_call(
        functools.partial(_attn_prompt_body, n_cmp, n_sel),
        grid=(bsz, N_KV_HEADS, t // tq),
        in_specs=[q_spec, q_spec, pl.BlockSpec((1, gate_rows, tq), lambda b, g, i: (b, g, i)), per_group(kc), per_group(vct),
                  pl.BlockSpec(c2st.shape, lambda b, g, i: (0, 0)), per_group(kaug), per_group(vst), per_group(kwin),
                  per_group(vwt)],
        out_specs=pl.BlockSpec((1, tq, gw), lambda b, g, i: (b, i, g)),
        out_shape=jax.ShapeDtypeStruct((bsz, t, Q_WIDTH), bf16),
        scratch_shapes=[pltpu.VMEM((LANES, n4), bf16), pltpu.VMEM((sel_rows, n4), f32)],
        compiler_params=_cparams(("arbitrary", "arbitrary", "arbitrary")),
        name="attn_prompt",
    )(qt_raw, qt_rot, gates_t, kc, vct, c2st, kaug, vst, kwin, vwt)


def _attn_sample_body(n_cmp, n_sel, win_buf, emit_win, *refs):
    for i in range(refs[0].shape[0]):
        _attn_sample_one(n_cmp, n_sel, win_buf, emit_win, i, *refs)


def _attn_sample_one(n_cmp, n_sel, win_buf, emit_win, i, qraw_ref, qrot_ref, gate_ref, kc_ref, vc_ref, c2s_ref, past_ref,
                     new_ref, wstate_ref, wnew_ref, o_ref, *win_out):
    w = KV_WIDTH
    q_raw = qraw_ref[i]
    q_rot = qrot_ref[i]
    n_head = q_raw.shape[0]
    head = lax.broadcasted_iota(jnp.int32, (n_head, n_head), 0)
    head2 = lax.broadcasted_iota(jnp.int32, (n_head, n_head), 1)
    same_group = (head // GROUP == head2 // GROUP).astype(bf16)

    s = _mm_nt(q_raw, kc_ref[i])
    blk = lax.broadcasted_iota(jnp.int32, s.shape, 1)
    mask = blk < n_cmp
    s = jnp.where(mask, s, NEG_INF)
    p = jnp.where(mask, jnp.exp(s - jnp.max(s, axis=1, keepdims=True)), 0.0)
    p = p / jnp.sum(p, axis=1, keepdims=True)
    o_c = jnp.dot(p.astype(bf16), vc_ref[i], preferred_element_type=f32)
    imp = _mm_exact_rhs(_mm_exact_lhs(same_group, p), c2s_ref[...])
    cur = jnp.full((n_head, 1), n_sel - 1, jnp.int32)
    sel = _topk_mask(imp, cur, n_sel)

    def attend(q, parts, mask_past, k_new, v_new, new_ok):
        sp = jnp.concatenate([_mm_nt(q, ref[j, :, 0:w].astype(bf16)) for ref, j in parts], axis=1)
        sp = jnp.where(mask_past, sp, NEG_INF)
        sn = jnp.sum(q.astype(f32) * k_new.astype(bf16).astype(f32), axis=1, keepdims=True)
        sn = jnp.where(new_ok, sn, NEG_INF)
        m = jnp.maximum(jnp.max(sp, axis=1, keepdims=True), sn)
        pp = jnp.where(mask_past, jnp.exp(sp - m), 0.0)
        pn = jnp.where(new_ok, jnp.exp(sn - m), 0.0)
        l = jnp.sum(pp, axis=1, keepdims=True) + pn
        pp = pp.astype(bf16)
        acc = pn.astype(bf16).astype(f32) * v_new.astype(bf16).astype(f32)
        rows = 0
        for ref, j in parts:
            n = ref.shape[1]
            acc = acc + jnp.dot(pp[:, rows:rows + n], ref[j, :, w:2 * w].astype(bf16), preferred_element_type=f32)
            rows += n
        return acc / l

    n_past = past_ref.shape[1]
    jj = lax.broadcasted_iota(jnp.int32, (LANES, n_past), 0)
    kk = lax.broadcasted_iota(jnp.int32, (LANES, n_past), 1)
    expand = (jj == kk // L_SEL).astype(bf16)
    picked = jnp.dot(sel.astype(bf16), expand, preferred_element_type=f32) > 0.5
    new_row = new_ref[i]
    o_s = attend(q_rot, [(past_ref, i)], picked, new_row[:, 2 * w:3 * w], new_row[:, 3 * w:4 * w],
                 sel[:, n_sel - 1:n_sel] > 0.5)

    idx = lax.broadcasted_iota(jnp.int32, (n_head, win_buf), 1)
    in_window = win_buf - idx < WINDOW
    wnew = wnew_ref[i]
    o_w = attend(q_rot, [(wstate_ref, i)], in_window, wnew[:, 0:w], wnew[:, w:2 * w], jnp.full((n_head, 1), True))

    gate = gate_ref[i]
    o = gate[:, 0:1] * o_c + gate[:, 1:2] * o_s + gate[:, 2:3] * o_w
    hrow = lax.broadcasted_iota(jnp.int32, o.shape, 0)
    lane = lax.broadcasted_iota(jnp.int32, o.shape, 1)
    o_ref[i] = jnp.where(lane // HEAD_DIM == hrow // GROUP, o, 0.0).astype(o_ref.dtype)

    if emit_win:
        state = wstate_ref[i]
        row = lax.broadcasted_iota(jnp.int32, state.shape, 0)
        win_out[0][i] = jnp.where(row == win_buf - 1, wnew, pltpu.roll(state, win_buf - 1, 0))


def _attn_sample(q_raw, q_rot, gates, kc, vc, c2s, past, new_rows, wstate, wnew, n_cmp, n_sel, emit_win):
    bsz = q_raw.shape[0]
    n_past = past.shape[1]
    win_buf = wstate.shape[1]
    n_seq = SEQ_PER_STEP if bsz % SEQ_PER_STEP == 0 else 1
    per_seq = lambda a: pl.BlockSpec((n_seq,) + a.shape[1:], lambda b: (b,) + (0,) * (a.ndim - 1))
    out_specs = [pl.BlockSpec((n_seq, N_HEADS, KV_WIDTH), lambda b: (b, 0, 0))]
    out_shape = [jax.ShapeDtypeStruct((bsz, N_HEADS, KV_WIDTH), bf16)]
    if emit_win:
        out_specs.append(per_seq(wstate))
        out_shape.append(jax.ShapeDtypeStruct(wstate.shape, wstate.dtype))
    return pl.pallas_call(
        functools.partial(_attn_sample_body, n_cmp, n_sel, win_buf, emit_win),
        grid=(bsz // n_seq,),
        in_specs=[per_seq(q_raw), per_seq(q_rot), per_seq(gates), per_seq(kc), per_seq(vc),
                  pl.BlockSpec(c2s.shape, lambda b: (0, 0)),
                  pl.BlockSpec((n_seq, n_past, 2 * KV_WIDTH), lambda b: (b, 0, 1)), per_seq(new_rows), per_seq(wstate),
                  per_seq(wnew)],
        out_specs=out_specs,
        out_shape=out_shape,
        compiler_params=_cparams(("arbitrary",)),
        name="attn_sample",
    )(q_raw, q_rot, gates, kc, vc, c2s, past, new_rows, wstate, wnew)


def _rope_tables(pos):
    half = ROT_DIM // 2
    inv = np.power(ROPE_THETA, -np.arange(half, dtype=np.float64) * (2.0 / ROT_DIM))
    ang = np.asarray(pos, np.float64)[:, None] * inv[None, :]
    cos_h = np.ones((len(pos), HEAD_DIM), np.float32)
    sin_h = np.zeros((len(pos), HEAD_DIM), np.float32)
    cos_h[:, :half] = np.cos(ang)
    cos_h[:, half:ROT_DIM] = np.cos(ang)
    sin_h[:, :half] = -np.sin(ang)
    sin_h[:, half:ROT_DIM] = np.sin(ang)
    return (jnp.asarray(np.tile(cos_h, (1, 2))), jnp.asarray(np.tile(sin_h, (1, 2))),
            jnp.asarray(cos_h.T.copy()), jnp.asarray(sin_h.T.copy()))


def _cmp_to_sel(t, n_rows, transposed):
    nc = (t - L_CMP) // D_CMP + 1
    ns = -(-t // L_SEL)
    c_start = np.arange(nc) * D_CMP
    s_start = np.arange(ns) * L_SEL
    m = np.zeros((n_rows, LANES), np.float32)
    m[:nc, :ns] = (c_start[:, None] < s_start[None, :] + L_SEL) & (c_start[:, None] + L_CMP > s_start[None, :])
    return jnp.asarray(m.T.copy() if transposed else m, bf16), nc, ns


def _gate_blockdiag(w_gate):
    wg = w_gate.astype(bf16).reshape(2, LRU_HEADS // 2, 2, LRU_BLOCK, LRU_BLOCK)
    z = jnp.zeros_like(wg[:, :, 0])
    top = jnp.concatenate([wg[:, :, 0], z], axis=-1)
    bot = jnp.concatenate([z, wg[:, :, 1]], axis=-1)
    return jnp.concatenate([top, bot], axis=-2)


def _compress_weights(pe_cmp, w_cmp1, b_cmp1, w_cmp2, per_group):
    pe = pe_cmp.reshape(2, 1, L_CMP * HEAD_DIM)
    w1 = w_cmp1.astype(bf16)
    w1p = w1.reshape(2, 2, D_CMP // 2, 2, HEAD_DIM, CMP_HIDDEN)
    eye2 = jnp.eye(2, dtype=bf16)
    wp = jnp.einsum('shpldk,ab->spladbhk', w1p, eye2).reshape(2, D_CMP // 2, 2 * LANES, 4 * CMP_HIDDEN)
    b1 = b_cmp1.reshape(2, 1, CMP_HIDDEN)
    w2 = w_cmp2.astype(bf16)
    eye = jnp.eye(N_KV_HEADS, dtype=bf16)
    if per_group:
        pad = jnp.zeros((CMP_HIDDEN, CMP_HIDDEN - HEAD_DIM), bf16)
        w2x = jnp.stack([jnp.concatenate([w2[0], pad], axis=1), jnp.concatenate([w2[1].T, pad.T], axis=0)])
    else:
        w2x = jnp.einsum('skd,gq->sgkqd', w2, eye).reshape(2, N_KV_HEADS, CMP_HIDDEN, KV_WIDTH)
    return pe, w1, wp, b1, w2x


def _q_weights_prompt(w_qg):
    wqt = w_qg[:, :Q_WIDTH].T.astype(bf16)
    wg = w_qg[:, Q_WIDTH:].T.reshape(N_KV_HEADS, 3 * GROUP, D_MODEL)
    wgt = jnp.pad(wg, ((0, 0), (0, 2 * SUBLANES - 3 * GROUP), (0, 0))).reshape(N_KV_HEADS * 2 * SUBLANES, D_MODEL).astype(bf16)
    return wqt, wgt


def _q_weights_sample(w_qg):
    wq = w_qg[:, :Q_WIDTH].astype(bf16).reshape(D_MODEL, N_KV_HEADS, GROUP, HEAD_DIM)
    eye = jnp.eye(N_KV_HEADS, dtype=bf16)
    wq = jnp.einsum('kgrd,gq->kgrqd', wq, eye).reshape(D_MODEL, N_HEADS * KV_WIDTH)
    wg = w_qg[:, Q_WIDTH:].reshape(D_MODEL, N_HEADS, 3)
    wg = jnp.pad(wg, ((0, 0), (0, 0), (0, LANES - 3))).reshape(D_MODEL, N_HEADS * LANES).astype(bf16)
    return wq, wg


def _out_weights_sample(w_o):
    w = w_o.astype(bf16).reshape(N_KV_HEADS, GROUP, HEAD_DIM, D_MODEL)
    eye = jnp.eye(N_KV_HEADS, dtype=bf16)
    return jnp.einsum('grdn,gq->grqdn', w, eye).reshape(N_HEADS * KV_WIDTH, D_MODEL)


def kernel(x_prompt, x_sample, c_prompt, c_sample, cache_nsa_kv, page_table, state_nsa_win, state_lru_h, state_lru_conv, state_ffn_conv, w_mod, b_mod, g_norm, w_lru_in, b_lru_in, w_lru_conv, b_lru_conv, w_lru_gate, b_lru_gate, lru_lambda, w_lru_out, g_kv, w_mod_kv, b_mod_kv, w_kv, pe_cmp, w_cmp1, b_cmp1, w_cmp2, w_nsa_qg, w_nsa_out, w_ffn_in, w_ffn_conv, b_ffn_conv, w_ffn_out):
    bp, tp, d = x_prompt.shape
    bs, ts, _ = x_sample.shape
    assert ts == 1 and d == D_MODEL
    depth = w_mod.shape[0]
    n_a = w_lru_in.shape[0]
    past_len = page_table.shape[1] * PAGE_SIZE
    win_buf = state_nsa_win.shape[1]

    c_all = jnp.concatenate([c_prompt, c_sample], axis=0)
    mod_all = _ada_mod(c_all, w_mod, b_mod).reshape(depth, bp + bs, N_MOD, d)
    mod_p = mod_all[:, :bp]
    mod_s = jnp.swapaxes(mod_all[:, bp:], 1, 2)
    modkv_all = _ada_mod(c_all, w_mod_kv[None], b_mod_kv[None]).reshape(bp + bs, 2, d)
    modkv_p = modkv_all[:bp]
    modkv_s = jnp.swapaxes(modkv_all[bp:], 0, 1)

    cos_p, sin_p, cos_pt, sin_pt = _rope_tables(np.arange(tp))
    cos_s, sin_s, _, _ = _rope_tables(np.asarray([past_len]))

    x_p = x_prompt
    x_s = x_sample.reshape(bs, d)
    lru_h_p, lru_h_s, lru_c_p, lru_c_s, ffn_c_p, ffn_c_s = [], [], [], [], [], []
    row1 = lambda a: a.reshape(1, -1)

    ctx_p = ctx_s = win_next_s = None
    keep_s = min(WINDOW, past_len + ts)
    for l in range(depth):
        g = g_norm[l]
        wi = w_ffn_in[l].astype(bf16)
        wo = w_ffn_out[l].astype(bf16)
        ffn_taps_s = jnp.swapaxes(state_ffn_conv[l], 0, 1)
        pre_p = pre_s = None
        if l < n_a:
            w_in = w_lru_in[l].astype(bf16)
            w_out = w_lru_out[l].astype(bf16)
            wg = _gate_blockdiag(w_lru_gate[l])
            args = (w_in, row1(b_lru_in[l]), w_lru_conv[l], row1(b_lru_conv[l]), wg, b_lru_gate[l], row1(lru_lambda[l]), w_out)
            x_p, h_last, cbuf = _lru_prompt(x_p, mod_p[l], g, *args)
            lru_h_p.append(h_last[:, 0])
            lru_c_p.append(cbuf[:, SUBLANES - (LRU_CONV_W - 1):])
            taps = jnp.swapaxes(state_lru_conv[l], 0, 1)
            x_s, h_new, up_new = _lru_sample(x_s, mod_s[l], g, state_lru_h[l], taps, *args)
            lru_h_s.append(h_new)
            lru_c_s.append(jnp.concatenate([state_lru_conv[l][:, 1:], up_new[:, None]], axis=1))
        else:
            j = l - n_a
            wq_p, wgt_p = _q_weights_prompt(w_nsa_qg[j])
            q_raw, q_rot, gates = _q_prompt(x_p, mod_p[l], g, wq_p, wgt_p, cos_pt, sin_pt)
            o_att = _attn_prompt(q_raw, q_rot, gates, *ctx_p)
            pre_p = (o_att, w_nsa_out[j].astype(bf16))
            wq_s, wgt_s = _q_weights_sample(w_nsa_qg[j])
            q_raw_s, q_rot_s, gates_s = _q_sample(x_s, mod_s[l], g, wq_s, wgt_s, cos_s, sin_s)
            emit_win = j == 0 and win_buf == keep_s
            res = _attn_sample(q_raw_s.reshape(bs, N_HEADS, KV_WIDTH), q_rot_s.reshape(bs, N_HEADS, KV_WIDTH),
                               gates_s.reshape(bs, N_HEADS, LANES), *ctx_s, emit_win)
            if emit_win:
                win_next_s = res[1]
            pre_s = (res[0].reshape(bs, N_HEADS * KV_WIDTH), _out_weights_sample(w_nsa_out[j]))
        x_p, fbuf = _ffn_prompt(x_p, mod_p[l], g, wi, w_ffn_conv[l], row1(b_ffn_conv[l]), wo, pre=pre_p)
        ffn_c_p.append(fbuf[:, SUBLANES - (FFN_CONV_W - 1):])
        x_s, z_new = _ffn_sample(x_s, mod_s[l], g, ffn_taps_s, wi, w_ffn_conv[l], row1(b_ffn_conv[l]), wo, pre=pre_s)
        ffn_c_s.append(jnp.concatenate([state_ffn_conv[l][:, 1:], z_new[:, None]], axis=1))

        if l == n_a - 1:
            w_kv_b = w_kv.astype(bf16)
            w_vt = jnp.concatenate([w_kv[:, 3 * KV_WIDTH:4 * KV_WIDTH].T, w_kv[:, 5 * KV_WIDTH:6 * KV_WIDTH].T], axis=0).astype(bf16)
            rows_p, win_p, kaug, kwin, vst, vwt = _kv_prompt(x_p, modkv_p, row1(g_kv), w_kv_b, w_vt, cos_p, sin_p)
            kc_p, vct_p = _compress(rows_p, _compress_weights(pe_cmp, w_cmp1, b_cmp1, w_cmp2, True), True)
            c2st_p, nc_p, ns_p = _cmp_to_sel(tp, kc_p.shape[2], True)
            ctx_p = (kc_p, vct_p, c2st_p, kaug, vst, kwin, vwt, nc_p, ns_p)
            rows_s, win_s = _kv_sample(x_s, modkv_s, row1(g_kv), w_kv_b, cos_s, sin_s)
            past = cache_nsa_kv[page_table].reshape(bs, past_len, 4 * KV_WIDTH)
            kc_s, vc_s = _compress(past, _compress_weights(pe_cmp, w_cmp1, b_cmp1, w_cmp2, False), False)
            c2s_s, nc_s, ns_s = _cmp_to_sel(past_len + ts, kc_s.shape[1], False)
            ctx_s = (kc_s, vc_s, c2s_s, past, rows_s.reshape(bs, 1, 4 * KV_WIDTH),
                     state_nsa_win.reshape(bs, win_buf, 2 * KV_WIDTH), win_s.reshape(bs, 1, 2 * KV_WIDTH), nc_s, ns_s)

    keep_p = min(WINDOW, tp)
    kv_rows_p = rows_p.reshape(bp, tp, 4, N_KV_HEADS, HEAD_DIM)
    win_out_p = win_p[:, tp - keep_p:].reshape(bp, keep_p, 2, N_KV_HEADS, HEAD_DIM)
    kv_rows_s = rows_s.reshape(bs, ts, 4, N_KV_HEADS, HEAD_DIM)
    if win_next_s is not None:
        win_out_s = win_next_s.reshape(bs, keep_s, 2, N_KV_HEADS, HEAD_DIM)
    else:
        win_all = jnp.concatenate([state_nsa_win, win_s.reshape(bs, ts, 2, N_KV_HEADS, HEAD_DIM)], axis=1)
        win_out_s = win_all[:, win_buf + ts - keep_s:]
    return (x_p, x_s.reshape(bs, ts, d), kv_rows_p, kv_rows_s, win_out_p, win_out_s,
            jnp.stack(lru_h_p), jnp.stack(lru_h_s), jnp.stack(lru_c_p), jnp.stack(lru_c_s),
            jnp.stack(ffn_c_p), jnp.stack(ffn_c_s))
```

```python
import functools

import numpy as np
import jax
import jax.numpy as jnp
from jax import lax
from jax.experimental import pallas as pl
from jax.experimental.pallas import tpu as pltpu

f32 = jnp.float32
bf16 = jnp.bfloat16

D_MODEL = 1024
D_RNN = D_MODEL
LRU_HEADS = 8
LRU_BLOCK = D_RNN // LRU_HEADS
LRU_CONV_W = 4
LRU_C = 8.0
N_HEADS = 16
HEAD_DIM = 64
N_KV_HEADS = 4
GROUP = N_HEADS // N_KV_HEADS
Q_WIDTH = N_HEADS * HEAD_DIM
KV_WIDTH = N_KV_HEADS * HEAD_DIM
L_CMP = 32
D_CMP = 16
CMP_HIDDEN = 128
L_SEL = 64
TOP_N = 8
WINDOW = 512
ROT_DIM = HEAD_DIM // 4
ROPE_THETA = 500000.0
D_FF = 3 * D_MODEL
FFN_CONV_W = 3
N_MOD = 6
EPS = 1e-6
NEG_INF = -1e30
FORCED_SCORE = 1e6
PAGE_SIZE = 128

LANES = 128
SUBLANES = 8
VMEM_LIMIT = 60 * 1024 * 1024
ROW_TILE = 256
FFN_TILE = 512
Q_TILE = 512
K_CHUNK = 512
FF_CHUNK = 1024
BELOW_ALL = -3.0e38
SEQ_PER_STEP = 2
SEL_GROUP = 16


def _cparams(sem=None):
    return pltpu.CompilerParams(dimension_semantics=sem, vmem_limit_bytes=VMEM_LIMIT)


def _const_spec(shape):
    nd = len(shape)
    return pl.BlockSpec(shape, lambda *_: (0,) * nd, pipeline_mode=pl.Buffered(1))


def _rms(x, g):
    return x * lax.rsqrt(jnp.mean(x * x, axis=-1, keepdims=True) + EPS) * g


def _gelu(x):
    return 0.5 * x * (1.0 + jnp.tanh(0.7978845608028654 * (x + 0.044715 * (x * x * x))))


def _mm(a, w):
    return jnp.dot(a.astype(bf16), w, preferred_element_type=f32)


def _mm_nt(a, b):
    return lax.dot_general(a, b, (((1,), (1,)), ((), ())), preferred_element_type=f32)


def _mm_exact_rhs(a, b01):
    a1 = a.astype(bf16)
    r1 = a - a1.astype(f32)
    a2 = r1.astype(bf16)
    a3 = (r1 - a2.astype(f32)).astype(bf16)
    return (jnp.dot(a1, b01, preferred_element_type=f32) + jnp.dot(a2, b01, preferred_element_type=f32)
            + jnp.dot(a3, b01, preferred_element_type=f32))


def _mm_exact_lhs(a01, b):
    b1 = b.astype(bf16)
    r1 = b - b1.astype(f32)
    b2 = r1.astype(bf16)
    b3 = (r1 - b2.astype(f32)).astype(bf16)
    return (jnp.dot(a01, b1, preferred_element_type=f32) + jnp.dot(a01, b2, preferred_element_type=f32)
            + jnp.dot(a01, b3, preferred_element_type=f32))


def _shift_rows(cur, tail, k):
    r = pltpu.roll(cur, k, 0)
    rp = pltpu.roll(tail, k, 0)
    row = lax.broadcasted_iota(jnp.int32, tail.shape, 0)
    first = jnp.where(row < k, rp, r[0:SUBLANES])
    return jnp.concatenate([first, r[SUBLANES:]], axis=0)


def _causal_conv_tile(cur, tail, w, b):
    width = w.shape[0]
    out = b + w[width - 1:width] * cur
    for k in range(1, width):
        out = out + w[width - 1 - k:width - k] * _shift_rows(cur, tail, k)
    return out


def _lru_gates(u, wg_ref, bg, lam):
    pair = 2 * LRU_BLOCK
    n_pair = D_RNN // pair
    ub = u.astype(bf16)
    gates = []
    for gi in range(2):
        cols = [jnp.dot(ub[:, p * pair:(p + 1) * pair], wg_ref[gi, p], preferred_element_type=f32) for p in range(n_pair)]
        gates.append(jnp.concatenate(cols, axis=1) + bg[gi:gi + 1])
    r = jax.nn.sigmoid(gates[0])
    i = jax.nn.sigmoid(gates[1])
    z = -lam
    softplus = jnp.maximum(z, 0.0) + jnp.log1p(jnp.exp(-jnp.abs(z)))
    log_a = -LRU_C * r * softplus
    a = jnp.exp(log_a)
    mult = jnp.sqrt(-jnp.tanh(log_a) * (a * a + 1.0))
    return a, mult * (i * u)


def _scan_rows(a, b, h0):
    n, c = a.shape
    n_grp = n // SUBLANES
    a = a.reshape(n_grp, SUBLANES, c)
    b = b.reshape(n_grp, SUBLANES, c)
    row = lax.broadcasted_iota(jnp.int32, a.shape, 1)
    s = 1
    while s < SUBLANES:
        a_sh = jnp.where(row >= s, pltpu.roll(a, s, 1), 1.0)
        b_sh = jnp.where(row >= s, pltpu.roll(b, s, 1), 0.0)
        b = a * b_sh + b
        a = a * a_sh
        s *= 2
    out = []
    carry = h0
    for i in range(n_grp):
        h = b[i] + a[i] * carry
        out.append(h)
        carry = h[SUBLANES - 1:SUBLANES]
    return jnp.concatenate(out, axis=0)


def _rope_cols(x, cos_t, sin_t):
    lane = lax.broadcasted_iota(jnp.int32, x.shape, 1)
    half = ROT_DIM // 2
    swapped = jnp.where(lane % HEAD_DIM < half, pltpu.roll(x, LANES - half, 1), pltpu.roll(x, half, 1))
    return x * cos_t + swapped * sin_t


def _rope_wide(x, cos_t, sin_t):
    n = x.shape[1] // LANES
    return jnp.concatenate([_rope_cols(x[:, j * LANES:(j + 1) * LANES], cos_t, sin_t) for j in range(n)], axis=1)


def _topk_mask(imp, cur, n_blocks):
    blk = lax.broadcasted_iota(jnp.int32, imp.shape, 1)
    forced = (blk == 0) | (blk == cur) | (blk == cur - 1)
    imp = jnp.where(forced, FORCED_SCORE, imp)
    imp = jnp.where(blk <= cur, imp, NEG_INF)
    v = jnp.where(blk < n_blocks, imp, BELOW_ALL)
    rank = jnp.zeros(imp.shape, jnp.int32)
    for i in range(n_blocks):
        vi = v[:, i:i + 1]
        rank = rank + ((vi > v) | ((vi == v) & (blk > i))).astype(jnp.int32)
    return jnp.where((rank < TOP_N) & (v > 0.5 * NEG_INF), 1.0, 0.0)


def _mod_body(c_ref, w_ref, b_ref, o_ref):
    c = c_ref[...]
    o_ref[0] = _mm(c * jax.nn.sigmoid(c), w_ref[0].astype(bf16)) + b_ref[0]


def _ada_mod(c, w, b):
    n_l, d, n = w.shape
    m = c.shape[0]
    tn = n // 2 if n % (2 * LANES) == 0 else n
    return pl.pallas_call(
        _mod_body,
        grid=(n_l, n // tn),
        in_specs=[pl.BlockSpec((m, d), lambda l, j: (0, 0)),
                  pl.BlockSpec((1, d, tn), lambda l, j: (l, 0, j)),
                  pl.BlockSpec((1, 1, tn), lambda l, j: (l, 0, j))],
        out_specs=pl.BlockSpec((1, m, tn), lambda l, j: (l, 0, j)),
        out_shape=jax.ShapeDtypeStruct((n_l, m, n), f32),
        compiler_params=_cparams(("arbitrary", "arbitrary")),
        name="ada_mod",
    )(c, w, b.reshape(n_l, 1, n))


def _lru_prompt_body(x_ref, mod_ref, g_ref, win_ref, bin_ref, wc_ref, bc_ref, wg_ref, bg_ref, lam_ref, wout_ref,
                     xo_ref, hlast_ref, cbuf_ref, tail_ref, h_ref):
    t = pl.program_id(1)

    @pl.when(t == 0)
    def _():
        tail_ref[...] = jnp.zeros(tail_ref.shape, f32)
        h_ref[...] = jnp.zeros(h_ref.shape, f32)

    x = x_ref[0]
    tm = x.shape[0]
    mod = mod_ref[0]
    g = g_ref[...]
    h = _rms(x, g[0:1]) * (1.0 + mod[1:2]) + mod[0:1]
    proj = _mm(h, win_ref[...]) + bin_ref[...]
    y = _gelu(proj[:, :D_RNN])
    up = proj[:, D_RNN:]
    u = _causal_conv_tile(up, tail_ref[...], wc_ref[...], bc_ref[...])
    last_rows = up[tm - SUBLANES:tm]
    tail_ref[...] = last_rows
    cbuf_ref[0] = last_rows
    a, bx = _lru_gates(u, wg_ref, bg_ref[...], lam_ref[...])
    hs = _scan_rows(a, bx, h_ref[0:1])
    h_fin = hs[tm - 1:tm]
    h_ref[...] = jnp.broadcast_to(h_fin, h_ref.shape)
    hlast_ref[0] = jnp.broadcast_to(h_fin, hlast_ref.shape[1:])
    out = _mm(hs * y, wout_ref[...])
    xo_ref[0] = x + mod[2:3] * _rms(out, g[1:2])


def _lru_prompt(x, mod, g, w_in, b_in, w_conv, b_conv, wg, bg, lam, w_out):
    bsz, t, d = x.shape
    tm = min(ROW_TILE, t)
    row_spec = pl.BlockSpec((1, tm, d), lambda b, i: (b, i, 0))
    small = lambda r, c: pl.BlockSpec((1, r, c), lambda b, i: (b, 0, 0))
    return pl.pallas_call(
        _lru_prompt_body,
        grid=(bsz, t // tm),
        in_specs=[row_spec, small(N_MOD, d), _const_spec(g.shape), _const_spec(w_in.shape), _const_spec(b_in.shape),
                  _const_spec(w_conv.shape), _const_spec(b_conv.shape), _const_spec(wg.shape), _const_spec(bg.shape),
                  _const_spec(lam.shape), _const_spec(w_out.shape)],
        out_specs=[row_spec, small(SUBLANES, D_RNN), small(SUBLANES, D_RNN)],
        out_shape=[jax.ShapeDtypeStruct((bsz, t, d), f32), jax.ShapeDtypeStruct((bsz, SUBLANES, D_RNN), f32),
                   jax.ShapeDtypeStruct((bsz, SUBLANES, D_RNN), f32)],
        scratch_shapes=[pltpu.VMEM((SUBLANES, D_RNN), f32), pltpu.VMEM((SUBLANES, D_RNN), f32)],
        compiler_params=_cparams(("arbitrary", "arbitrary")),
        name="lru_prompt",
    )(x, mod, g, w_in, b_in, w_conv, b_conv, wg, bg, lam, w_out)


def _lru_sample_body(x_ref, mod_ref, g_ref, h0_ref, taps_ref, win_ref, bin_ref, wc_ref, bc_ref, wg_ref, bg_ref, lam_ref,
                     wout_ref, xo_ref, hnew_ref, up_ref):
    x = x_ref[...]
    g = g_ref[...]
    h = _rms(x, g[0:1]) * (1.0 + mod_ref[1]) + mod_ref[0]
    proj = _mm(h, win_ref[...]) + bin_ref[...]
    y = _gelu(proj[:, :D_RNN])
    up = proj[:, D_RNN:]
    up_ref[...] = up
    wc = wc_ref[...]
    u = bc_ref[...] + wc[LRU_CONV_W - 1:LRU_CONV_W] * up
    for k in range(LRU_CONV_W - 1):
        u = u + wc[k:k + 1] * taps_ref[k]
    a, bx = _lru_gates(u, wg_ref, bg_ref[...], lam_ref[...])
    hs = a * h0_ref[...] + bx
    hnew_ref[...] = hs
    out = _mm(hs * y, wout_ref[...])
    xo_ref[...] = x + mod_ref[2] * _rms(out, g[1:2])


def _lru_sample(x, mod, g, h0, taps, w_in, b_in, w_conv, b_conv, wg, bg, lam, w_out):
    m, d = x.shape
    return pl.pallas_call(
        _lru_sample_body,
        out_shape=[jax.ShapeDtypeStruct((m, d), f32), jax.ShapeDtypeStruct((m, D_RNN), f32),
                   jax.ShapeDtypeStruct((m, D_RNN), f32)],
        compiler_params=_cparams(),
        name="lru_sample",
    )(x, mod, g, h0, taps, w_in, b_in, w_conv, b_conv, wg, bg, lam, w_out)


def _ffn_core(h, tap_fn, wi_ref, wc, bc, wo_ref, store_fn):
    acc = jnp.zeros((h.shape[0], D_MODEL), f32)
    hb = h.astype(bf16)
    for c in range(D_FF // FF_CHUNK):
        zs = []
        for base in (0, D_FF):
            lo = base + c * FF_CHUNK
            z = jnp.dot(hb, wi_ref[:, lo:lo + FF_CHUNK], preferred_element_type=f32)
            store_fn(z, lo)
            zc = bc[:, lo:lo + FF_CHUNK] + wc[FFN_CONV_W - 1:FFN_CONV_W, lo:lo + FF_CHUNK] * z
            for k in range(1, FFN_CONV_W):
                zc = zc + wc[FFN_CONV_W - 1 - k:FFN_CONV_W - k, lo:lo + FF_CHUNK] * tap_fn(z, lo, k)
            zs.append(zc)
        gz = _gelu(zs[0]) * zs[1]
        acc = acc + jnp.dot(gz.astype(bf16), wo_ref[c * FF_CHUNK:(c + 1) * FF_CHUNK, :], preferred_element_type=f32)
    return acc


def _ffn_prompt_body(has_pre, *refs):
    if has_pre:
        (x_ref, o_ref, wpre_ref, mod_ref, g_ref, wi_ref, wc_ref, bc_ref, wo_ref, xo_ref, fbuf_ref, tail_ref) = refs
    else:
        (x_ref, mod_ref, g_ref, wi_ref, wc_ref, bc_ref, wo_ref, xo_ref, fbuf_ref, tail_ref) = refs
    t = pl.program_id(1)

    @pl.when(t == 0)
    def _():
        tail_ref[...] = jnp.zeros(tail_ref.shape, f32)

    x = x_ref[0]
    tm = x.shape[0]
    mod = mod_ref[0]
    g = g_ref[...]
    if has_pre:
        x = x + mod[2:3] * _rms(jnp.dot(o_ref[0], wpre_ref[...], preferred_element_type=f32), g[1:2])
    h = _rms(x, g[2:3]) * (1.0 + mod[4:5]) + mod[3:4]

    def tap(z, lo, k):
        return _shift_rows(z, tail_ref[:, lo:lo + FF_CHUNK], k)

    def store(z, lo):
        fbuf_ref[0, :, lo:lo + FF_CHUNK] = z[tm - SUBLANES:tm]

    acc = _ffn_core(h, tap, wi_ref, wc_ref[...], bc_ref[...], wo_ref, store)
    tail_ref[...] = fbuf_ref[0]
    xo_ref[0] = x + mod[5:6] * _rms(acc, g[3:4])


def _ffn_prompt(x, mod, g, w_in, w_conv, b_conv, w_out, pre=None):
    bsz, t, d = x.shape
    tm = min(FFN_TILE, t)
    row_spec = pl.BlockSpec((1, tm, d), lambda b, i: (b, i, 0))
    small = lambda r, c: pl.BlockSpec((1, r, c), lambda b, i: (b, 0, 0))
    ins, specs = [x], [row_spec]
    if pre is not None:
        o_att, w_pre = pre
        ins += [o_att, w_pre]
        specs += [pl.BlockSpec((1, tm, o_att.shape[2]), lambda b, i: (b, i, 0)), _const_spec(w_pre.shape)]
    ins += [mod, g, w_in, w_conv, b_conv, w_out]
    specs += [small(N_MOD, d), _const_spec(g.shape), _const_spec(w_in.shape), _const_spec(w_conv.shape),
              _const_spec(b_conv.shape), _const_spec(w_out.shape)]
    return pl.pallas_call(
        functools.partial(_ffn_prompt_body, pre is not None),
        grid=(bsz, t // tm),
        in_specs=specs,
        out_specs=[row_spec, small(SUBLANES, 2 * D_FF)],
        out_shape=[jax.ShapeDtypeStruct((bsz, t, d), f32), jax.ShapeDtypeStruct((bsz, SUBLANES, 2 * D_FF), f32)],
        scratch_shapes=[pltpu.VMEM((SUBLANES, 2 * D_FF), f32)],
        compiler_params=_cparams(("arbitrary", "arbitrary")),
        name="ffn_prompt",
    )(*ins)


def _ffn_sample_body(has_pre, *refs):
    if has_pre:
        (x_ref, o_ref, wpre_ref, mod_ref, g_ref, taps_ref, wi_ref, wc_ref, bc_ref, wo_ref, xo_ref, z_ref) = refs
    else:
        (x_ref, mod_ref, g_ref, taps_ref, wi_ref, wc_ref, bc_ref, wo_ref, xo_ref, z_ref) = refs
    x = x_ref[...]
    g = g_ref[...]
    if has_pre:
        x = x + mod_ref[2] * _rms(jnp.dot(o_ref[...], wpre_ref[...], preferred_element_type=f32), g[1:2])
    h = _rms(x, g[2:3]) * (1.0 + mod_ref[4]) + mod_ref[3]

    def tap(z, lo, k):
        return taps_ref[FFN_CONV_W - 1 - k, :, lo:lo + FF_CHUNK]

    def store(z, lo):
        z_ref[:, lo:lo + FF_CHUNK] = z

    acc = _ffn_core(h, tap, wi_ref, wc_ref[...], bc_ref[...], wo_ref, store)
    xo_ref[...] = x + mod_ref[5] * _rms(acc, g[3:4])


def _ffn_sample(x, mod, g, taps, w_in, w_conv, b_conv, w_out, pre=None):
    m, d = x.shape
    ins = [x] + (list(pre) if pre is not None else []) + [mod, g, taps, w_in, w_conv, b_conv, w_out]
    return pl.pallas_call(
        functools.partial(_ffn_sample_body, pre is not None),
        out_shape=[jax.ShapeDtypeStruct((m, d), f32), jax.ShapeDtypeStruct((m, 2 * D_FF), f32)],
        compiler_params=_cparams(),
        name="ffn_sample",
    )(*ins)


def _head_cols(src, extra):
    lane = lax.broadcasted_iota(jnp.int32, (src.shape[0], LANES), 1)
    out = []
    for j in range(KV_WIDTH // LANES):
        col = src[:, j * LANES:(j + 1) * LANES]
        out.append(jnp.where(lane < HEAD_DIM, col, extra))
        out.append(jnp.where(lane < HEAD_DIM, pltpu.roll(col, HEAD_DIM, 1), extra))
    return out


def _kv_prompt_body(x_ref, mod_ref, g_ref, w_ref, wvt_ref, cos_ref, sin_ref, rows_ref, win_ref, kaug_ref, kwin_ref, vst_ref, vwt_ref):
    x = x_ref[0]
    tm = x.shape[0]
    mod = mod_ref[0]
    h = _rms(x, g_ref[...]) * (1.0 + mod[1:2]) + mod[0:1]
    hb = h.astype(bf16)
    kv = jnp.dot(hb, w_ref[...], preferred_element_type=f32)
    vt = _mm_nt(wvt_ref[...], hb)
    cos_t, sin_t = cos_ref[...], sin_ref[...]
    w = KV_WIDTH
    k_sel = _rope_wide(kv[:, 2 * w:3 * w], cos_t, sin_t)
    k_win = _rope_wide(kv[:, 4 * w:5 * w], cos_t, sin_t)
    rows_ref[0, :, 0:2 * w] = kv[:, 0:2 * w]
    rows_ref[0, :, 2 * w:3 * w] = k_sel
    rows_ref[0, :, 3 * w:4 * w] = kv[:, 3 * w:4 * w]
    win_ref[0, :, 0:w] = k_win
    win_ref[0, :, w:2 * w] = kv[:, 5 * w:6 * w]
    row = lax.broadcasted_iota(jnp.int32, (tm, LANES), 0)
    lane = lax.broadcasted_iota(jnp.int32, (tm, LANES), 1)
    pos = pl.program_id(1) * tm + row
    onehot = jnp.where(lane - HEAD_DIM == (pos // L_SEL) % SEL_GROUP, 1.0, 0.0)
    for gi, col in enumerate(_head_cols(k_sel, onehot)):
        for c in range(tm // K_CHUNK):
            kaug_ref[0, gi, c] = col[c * K_CHUNK:(c + 1) * K_CHUNK].astype(bf16)
    for gi, col in enumerate(_head_cols(k_win, 0.0)):
        for c in range(tm // Q_TILE):
            kwin_ref[0, gi, c] = col[c * Q_TILE:(c + 1) * Q_TILE, 0:HEAD_DIM].astype(bf16)
    for gi in range(N_KV_HEADS):
        v_s = vt[gi * HEAD_DIM:(gi + 1) * HEAD_DIM].astype(bf16)
        v_w = vt[w + gi * HEAD_DIM:w + (gi + 1) * HEAD_DIM].astype(bf16)
        for c in range(tm // K_CHUNK):
            vst_ref[0, gi, c] = v_s[:, c * K_CHUNK:(c + 1) * K_CHUNK]
        for c in range(tm // Q_TILE):
            vwt_ref[0, gi, c] = v_w[:, c * Q_TILE:(c + 1) * Q_TILE]


def _kv_prompt(x, mod, g_kv, w_kv, w_vt, cos_t, sin_t):
    bsz, t, d = x.shape
    tm = K_CHUNK
    nc, nt = tm // K_CHUNK, tm // Q_TILE
    row = lambda c: pl.BlockSpec((1, tm, c), lambda b, i: (b, i, 0))
    chunked = lambda n, r, c: pl.BlockSpec((1, N_KV_HEADS, n, r, c), lambda b, i: (b, 0, i, 0, 0))
    shape5 = lambda n, r, c: jax.ShapeDtypeStruct((bsz, N_KV_HEADS, n, r, c), bf16)
    return pl.pallas_call(
        _kv_prompt_body,
        grid=(bsz, t // tm),
        in_specs=[row(d), pl.BlockSpec((1, 2, d), lambda b, i: (b, 0, 0)), _const_spec(g_kv.shape), _const_spec(w_kv.shape),
                  _const_spec(w_vt.shape), pl.BlockSpec((tm, LANES), lambda b, i: (i, 0)),
                  pl.BlockSpec((tm, LANES), lambda b, i: (i, 0))],
        out_specs=[row(4 * KV_WIDTH), row(2 * KV_WIDTH), chunked(nc, K_CHUNK, LANES), chunked(nt, Q_TILE, HEAD_DIM),
                   chunked(nc, HEAD_DIM, K_CHUNK), chunked(nt, HEAD_DIM, Q_TILE)],
        out_shape=[jax.ShapeDtypeStruct((bsz, t, 4 * KV_WIDTH), f32), jax.ShapeDtypeStruct((bsz, t, 2 * KV_WIDTH), f32),
                   shape5(t // K_CHUNK, K_CHUNK, LANES), shape5(t // Q_TILE, Q_TILE, HEAD_DIM),
                   shape5(t // K_CHUNK, HEAD_DIM, K_CHUNK), shape5(t // Q_TILE, HEAD_DIM, Q_TILE)],
        compiler_params=_cparams(("arbitrary", "arbitrary")),
        name="kv_prompt",
    )(x, mod, g_kv, w_kv, w_vt, cos_t, sin_t)


def _kv_sample_body(x_ref, mod_ref, g_ref, w_ref, cos_ref, sin_ref, rows_ref, win_ref):
    h = _rms(x_ref[...], g_ref[...]) * (1.0 + mod_ref[1]) + mod_ref[0]
    kv = _mm(h, w_ref[...])
    cos_t, sin_t = cos_ref[...], sin_ref[...]
    w = KV_WIDTH
    rows_ref[:, 0:2 * w] = kv[:, 0:2 * w]
    rows_ref[:, 2 * w:3 * w] = _rope_wide(kv[:, 2 * w:3 * w], cos_t, sin_t)
    rows_ref[:, 3 * w:4 * w] = kv[:, 3 * w:4 * w]
    win_ref[:, 0:w] = _rope_wide(kv[:, 4 * w:5 * w], cos_t, sin_t)
    win_ref[:, w:2 * w] = kv[:, 5 * w:6 * w]


def _kv_sample(x, mod, g_kv, w_kv, cos_t, sin_t):
    m = x.shape[0]
    return pl.pallas_call(
        _kv_sample_body,
        out_shape=[jax.ShapeDtypeStruct((m, 4 * KV_WIDTH), f32), jax.ShapeDtypeStruct((m, 2 * KV_WIDTH), f32)],
        compiler_params=_cparams(),
        name="kv_sample",
    )(x, mod, g_kv, w_kv, cos_t, sin_t)


def _compress_body(per_group, *refs):
    n_slab = 2 * KV_WIDTH // LANES
    slabs = refs[:n_slab]
    pe_ref, w1_ref, wp_ref, b1_ref, w2_ref = refs[n_slab:n_slab + 5]
    outs = refs[n_slab + 5:]
    n_chunk = slabs[0].shape[1] // D_CMP
    rows_of = lambda j, start: slabs[j][0, pl.ds(start, n_chunk, stride=D_CMP), :]
    per_slot = KV_WIDTH // LANES
    hid = CMP_HIDDEN
    for s in range(2):
        const = _mm(jnp.broadcast_to(pe_ref[s], (SUBLANES, L_CMP * HEAD_DIM)), w1_ref[s])[0:1] + b1_ref[s]
        tok = jnp.zeros((n_chunk, KV_WIDTH), f32)
        for j in range(per_slot):
            acc = jnp.zeros((n_chunk, 4 * hid), f32)
            for rp in range(D_CMP // 2):
                lhs = jnp.concatenate([rows_of(s * per_slot + j, 2 * rp), rows_of(s * per_slot + j, 2 * rp + 1)], axis=1)
                acc = acc + _mm(lhs, wp_ref[s, rp])
            for hl in range(2):
                gi = 2 * j + hl
                first = acc[:, 2 * hl * hid:(2 * hl + 1) * hid]
                second = acc[:, (2 * hl + 1) * hid:(2 * hl + 2) * hid]
                act_g = _gelu(first + pltpu.roll(second, n_chunk - 1, 0) + const).astype(bf16)
                if not per_group:
                    tok = tok + jnp.dot(act_g, w2_ref[s, gi], preferred_element_type=f32)
                elif s == 0:
                    outs[s][0, gi] = jnp.dot(act_g, w2_ref[s], preferred_element_type=f32)[:, 0:HEAD_DIM].astype(bf16)
                else:
                    outs[s][0, gi] = _mm_nt(w2_ref[s], act_g)[0:HEAD_DIM].astype(bf16)
        if not per_group:
            outs[s][0] = tok.astype(bf16)


def _compress(rows, cw, per_group):
    bsz, t = rows.shape[:2]
    n_chunk = t // D_CMP
    n_slab = 2 * KV_WIDTH // LANES
    slab_specs = [pl.BlockSpec((1, t, LANES), functools.partial(lambda j, b: (b, 0, j), j)) for j in range(n_slab)]
    pe, wt, wb, b1, w2 = cw
    if per_group:
        dims = [(n_chunk, HEAD_DIM), (HEAD_DIM, n_chunk)]
        out_specs = [pl.BlockSpec((1, N_KV_HEADS) + dm, lambda b: (b, 0, 0, 0)) for dm in dims]
        out_shapes = [jax.ShapeDtypeStruct((bsz, N_KV_HEADS) + dm, bf16) for dm in dims]
    else:
        out_specs = [pl.BlockSpec((1, n_chunk, KV_WIDTH), lambda b: (b, 0, 0))] * 2
        out_shapes = [jax.ShapeDtypeStruct((bsz, n_chunk, KV_WIDTH), bf16)] * 2
    return pl.pallas_call(
        functools.partial(_compress_body, per_group),
        grid=(bsz,),
        in_specs=slab_specs + [_const_spec(pe.shape), _const_spec(wt.shape), _const_spec(wb.shape), _const_spec(b1.shape),
                               _const_spec(w2.shape)],
        out_specs=out_specs,
        out_shape=out_shapes,
        compiler_params=_cparams(("arbitrary",)),
        name="compress_prompt" if per_group else "compress_sample",
    )(*([rows] * n_slab), pe, wt, wb, b1, w2)


def _q_prompt_body(x_ref, mod_ref, g_ref, wqt_ref, wgt_ref, cos_ref, sin_ref, qraw_ref, qrot_ref, gate_ref):
    mod = mod_ref[0]
    h = _rms(x_ref[0], g_ref[0:1]) * (1.0 + mod[1:2]) + mod[0:1]
    hb = h.astype(bf16)
    qt = _mm_nt(wqt_ref[...], hb)
    gate_ref[0] = jax.nn.sigmoid(_mm_nt(wgt_ref[...], hb))
    scale = HEAD_DIM ** -0.5
    qraw_ref[0] = (qt * scale).astype(bf16)
    half = ROT_DIM // 2
    dim = lax.broadcasted_iota(jnp.int32, qt.shape, 0) % HEAD_DIM
    swapped = jnp.where(dim < half, pltpu.roll(qt, Q_WIDTH - half, 0), pltpu.roll(qt, half, 0))
    cos_t = jnp.concatenate([cos_ref[...]] * N_HEADS, axis=0)
    sin_t = jnp.concatenate([sin_ref[...]] * N_HEADS, axis=0)
    qrot_ref[0] = ((qt * cos_t + swapped * sin_t) * scale).astype(bf16)


def _q_prompt(x, mod, g, wqt, wgt, cos_tt, sin_tt):
    bsz, t, d = x.shape
    tm = min(ROW_TILE, t)
    col = lambda r: pl.BlockSpec((1, r, tm), lambda b, i: (b, 0, i))
    tab = pl.BlockSpec((HEAD_DIM, tm), lambda b, i: (0, i))
    return pl.pallas_call(
        _q_prompt_body,
        grid=(bsz, t // tm),
        in_specs=[pl.BlockSpec((1, tm, d), lambda b, i: (b, i, 0)), pl.BlockSpec((1, N_MOD, d), lambda b, i: (b, 0, 0)),
                  _const_spec(g.shape), _const_spec(wqt.shape), _const_spec(wgt.shape), tab, tab],
        out_specs=[col(Q_WIDTH), col(Q_WIDTH), col(wgt.shape[0])],
        out_shape=[jax.ShapeDtypeStruct((bsz, Q_WIDTH, t), bf16), jax.ShapeDtypeStruct((bsz, Q_WIDTH, t), bf16),
                   jax.ShapeDtypeStruct((bsz, wgt.shape[0], t), f32)],
        compiler_params=_cparams(("arbitrary", "arbitrary")),
        name="q_prompt",
    )(x, mod, g, wqt, wgt, cos_tt, sin_tt)


def _q_sample_body(x_ref, mod_ref, g_ref, wq_ref, wgt_ref, cos_ref, sin_ref, qraw_ref, qrot_ref, gate_ref):
    h = _rms(x_ref[...], g_ref[0:1]) * (1.0 + mod_ref[1]) + mod_ref[0]
    hb = h.astype(bf16)
    q = jnp.dot(hb, wq_ref[...], preferred_element_type=f32)
    gate_ref[...] = jax.nn.sigmoid(jnp.dot(hb, wgt_ref[...], preferred_element_type=f32))
    scale = HEAD_DIM ** -0.5
    qraw_ref[...] = (q * scale).astype(bf16)
    qrot_ref[...] = (_rope_wide(q, cos_ref[...], sin_ref[...]) * scale).astype(bf16)


def _q_sample(x, mod, g, wq, wgt, cos_t, sin_t):
    m = x.shape[0]
    return pl.pallas_call(
        _q_sample_body,
        out_shape=[jax.ShapeDtypeStruct((m, wq.shape[1]), bf16), jax.ShapeDtypeStruct((m, wq.shape[1]), bf16),
                   jax.ShapeDtypeStruct((m, wgt.shape[1]), f32)],
        compiler_params=_cparams(),
        name="q_sample",
    )(x, mod, g, wq, wgt, cos_t, sin_t)


def _rank_select(imp, cur, n_blocks):
    blk = lax.broadcasted_iota(jnp.int32, imp.shape, 0)
    forced = (blk == 0) | (blk == cur) | (blk == cur - 1)
    v = jnp.where(forced, FORCED_SCORE, imp)
    v = jnp.where(blk <= cur, v, NEG_INF)
    v = jnp.where(blk < n_blocks, v, BELOW_ALL)
    rank = jnp.zeros(imp.shape, jnp.int32)
    for i in range(n_blocks):
        vi = v[i:i + 1, :]
        rank = rank + ((vi > v) | ((vi == v) & (blk > i))).astype(jnp.int32)
    return (rank < TOP_N) & (v > 0.5 * NEG_INF)


def _attn_prompt_body(n_cmp, n_sel, qraw_ref, qrot_ref, gate_ref, kc_ref, vct_ref, c2st_ref, kaug_ref, vst_ref, kwin_ref,
                      vwt_ref, o_ref, qaug_ref, negsel_ref):
    qi = pl.program_id(2)
    tq = qraw_ref.shape[2]
    hq = tq // 2
    n4 = GROUP * tq
    nh = GROUP * hq
    q_chunk = qi

    def stack(ref):
        return jnp.concatenate([ref[0, r * HEAD_DIM:(r + 1) * HEAD_DIM, h * hq:(h + 1) * hq]
                                for h in range(2) for r in range(GROUP)], axis=1)

    def per_query(x):
        return jnp.concatenate([x[:, h * hq:(h + 1) * hq] for h in range(2) for _ in range(GROUP)], axis=1)

    q_raw = stack(qraw_ref)
    q_rot = stack(qrot_ref)
    lane = lax.broadcasted_iota(jnp.int32, (1, n4), 1)
    q_pos = qi * tq + (lane // nh) * hq + lane % hq

    s = jnp.dot(kc_ref[0, 0], q_raw, preferred_element_type=f32)
    blk = lax.broadcasted_iota(jnp.int32, s.shape, 0)
    mask = (blk * D_CMP + (L_CMP - 1) <= q_pos) & (blk < n_cmp)
    s = jnp.where(mask, s, NEG_INF)
    p = jnp.where(mask, jnp.exp(s - jnp.max(s, axis=0, keepdims=True)), 0.0)
    l = jnp.sum(p, axis=0, keepdims=True)
    p = p / jnp.where(l > 0.0, l, 1.0)
    o_c = jnp.dot(vct_ref[0, 0], p.astype(bf16), preferred_element_type=f32)

    halves = []
    for h in range(2):
        acc_h = p[:, h * nh:h * nh + hq]
        for r in range(1, GROUP):
            acc_h = acc_h + p[:, h * nh + r * hq:h * nh + (r + 1) * hq]
        halves.append(acc_h)
    p_group = jnp.concatenate(halves, axis=1)
    imp = _mm_exact_lhs(c2st_ref[...], p_group)
    n_rows = negsel_ref.shape[0]
    cur = (qi * tq + lax.broadcasted_iota(jnp.int32, (1, tq), 1)) // L_SEL
    picked = _rank_select(imp[0:n_rows], cur, n_sel)
    negsel_ref[...] = per_query(jnp.where(picked, 0.0, NEG_INF))

    qaug_ref[0:HEAD_DIM, :] = q_rot
    qaug_ref[HEAD_DIM + SEL_GROUP:, :] = jnp.zeros((LANES - HEAD_DIM - SEL_GROUP, n4), bf16)
    blocks_per_chunk = K_CHUNK // L_SEL

    def load_sel_rows(c):
        first = pl.multiple_of((c * blocks_per_chunk // SEL_GROUP) * SEL_GROUP, SEL_GROUP)
        qaug_ref[HEAD_DIM:HEAD_DIM + SEL_GROUP, :] = negsel_ref[pl.ds(first, SEL_GROUP), :].astype(bf16)

    def chunk_scores(c):
        load_sel_rows(c)
        return jnp.dot(kaug_ref[0, 0, c], qaug_ref[...], preferred_element_type=f32)

    def update(sc, carry, vt):
        m_old, l_old, acc_old = carry
        m_new = jnp.maximum(m_old, jnp.max(sc, axis=0, keepdims=True))
        alpha = jnp.exp(m_old - m_new)
        pc = jnp.exp(sc - m_new)
        l_new = alpha * l_old + jnp.sum(pc, axis=0, keepdims=True)
        acc_new = alpha * acc_old + jnp.dot(vt, pc.astype(bf16), preferred_element_type=f32)
        return m_new, l_new, acc_new

    init = (jnp.full((1, n4), NEG_INF, f32), jnp.zeros((1, n4), f32), jnp.zeros((HEAD_DIM, n4), f32))
    carry = lax.fori_loop(0, q_chunk, lambda c, cr: update(chunk_scores(c), cr, vst_ref[0, 0, c]), init)
    load_sel_rows(q_chunk)
    k_own = kaug_ref[0, 0, q_chunk]
    v_own = vst_ref[0, 0, q_chunk]
    l_parts, acc_parts = [], []
    for h in range(2):
        n_keys = (h + 1) * hq
        lanes = slice(h * nh, (h + 1) * nh)
        sc = jnp.dot(k_own[0:n_keys], qaug_ref[:, lanes], preferred_element_type=f32)
        key_pos = q_chunk * K_CHUNK + lax.broadcasted_iota(jnp.int32, sc.shape, 0)
        sc = jnp.where(key_pos <= q_pos[:, lanes], sc, NEG_INF)
        _, l_h, acc_h = update(sc, tuple(x[:, lanes] for x in carry), v_own[:, 0:n_keys])
        l_parts.append(l_h)
        acc_parts.append(acc_h)
    l_s = jnp.concatenate(l_parts, axis=1)
    acc_s = jnp.concatenate(acc_parts, axis=1)

    prev = jnp.maximum(qi - 1, 0)
    no_prev = jnp.where(qi >= 1, 0.0, NEG_INF)
    k_prev, k_cur = kwin_ref[0, 0, prev], kwin_ref[0, 0, qi]
    v_prev, v_cur = vwt_ref[0, 0, prev], vwt_ref[0, 0, qi]
    l_parts, acc_parts = [], []
    for h in range(2):
        lanes = slice(h * nh, (h + 1) * nh)
        q_h = q_rot[:, lanes]
        rows_prev = slice(h * hq, tq)
        rows_cur = slice(0, (h + 1) * hq)
        s_prev = jnp.dot(k_prev[rows_prev], q_h, preferred_element_type=f32) + no_prev
        s_cur = jnp.dot(k_cur[rows_cur], q_h, preferred_element_type=f32)
        col = h * hq + lax.broadcasted_iota(jnp.int32, (1, nh), 1) % hq
        row_prev = h * hq + lax.broadcasted_iota(jnp.int32, s_prev.shape, 0)
        row_cur = lax.broadcasted_iota(jnp.int32, s_cur.shape, 0)
        sw = jnp.concatenate([jnp.where(row_prev > col, s_prev, NEG_INF), jnp.where(row_cur <= col, s_cur, NEG_INF)], axis=0)
        pw = jnp.exp(sw - jnp.max(sw, axis=0, keepdims=True))
        l_parts.append(jnp.sum(pw, axis=0, keepdims=True))
        v_h = jnp.concatenate([v_prev[:, rows_prev], v_cur[:, rows_cur]], axis=1)
        acc_parts.append(jnp.dot(v_h, pw.astype(bf16), preferred_element_type=f32))
    l_w = jnp.concatenate(l_parts, axis=1)
    acc_w = jnp.concatenate(acc_parts, axis=1)

    gate = gate_ref[0]
    o_s = acc_s / l_s
    o_w = acc_w / l_w
    heads = []
    for r in range(GROUP):
        parts = []
        for h in range(2):
            sl = slice(h * nh + r * hq, h * nh + (r + 1) * hq)
            gt = gate[:, h * hq:(h + 1) * hq]
            parts.append(gt[3 * r:3 * r + 1] * o_c[:, sl] + gt[3 * r + 1:3 * r + 2] * o_s[:, sl]
                         + gt[3 * r + 2:3 * r + 3] * o_w[:, sl])
        heads.append(jnp.concatenate(parts, axis=1))
    o_ref[0] = jnp.concatenate(heads, axis=0).T.astype(o_ref.dtype)


def _attn_prompt(qt_raw, qt_rot, gates_t, kc, vct, c2st, kaug, vst, kwin, vwt, n_cmp, n_sel):
    bsz, _, t = qt_raw.shape
    tq = Q_TILE
    assert tq == K_CHUNK == WINDOW and t % tq == 0
    gw = GROUP * HEAD_DIM
    n4 = GROUP * tq
    q_spec = pl.BlockSpec((1, gw, tq), lambda b, g, i: (b, g, i))
    gate_rows = gates_t.shape[1] // N_KV_HEADS
    per_group = lambda a: pl.BlockSpec((1, 1) + a.shape[2:], lambda b, g, i: (b, g) + (0,) * (a.ndim - 2))
    sel_rows = -(-n_sel // SEL_GROUP) * SEL_GROUP
    return pl.pallas_call(
        functools.partial(_attn_prompt_body, n_cmp, n_sel),
        grid=(bsz, N_KV_HEADS, t // tq),
        in_specs=[q_spec, q_spec, pl.BlockSpec((1, gate_rows, tq), lambda b, g, i: (b, g, i)), per_group(kc), per_group(vct),
                  pl.BlockSpec(c2st.shape, lambda b, g, i: (0, 0)), per_group(kaug), per_group(vst), per_group(kwin),
                  per_group(vwt)],
        out_specs=pl.BlockSpec((1, tq, gw), lambda b, g, i: (b, i, g)),
        out_shape=jax.ShapeDtypeStruct((bsz, t, Q_WIDTH), bf16),
        scratch_shapes=[pltpu.VMEM((LANES, n4), bf16), pltpu.VMEM((sel_rows, n4), f32)],
        compiler_params=_cparams(("arbitrary", "arbitrary", "arbitrary")),
        name="attn_prompt",
    )(qt_raw, qt_rot, gates_t, kc, vct, c2st, kaug, vst, kwin, vwt)


def _attn_sample_body(n_cmp, n_sel, win_buf, emit_win, *refs):
    for i in range(refs[0].shape[0]):
        _attn_sample_one(n_cmp, n_sel, win_buf, emit_win, i, *refs)


def _attn_sample_one(n_cmp, n_sel, win_buf, emit_win, i, qraw_ref, qrot_ref, gate_ref, kc_ref, vc_ref, c2s_ref, past_ref,
                     new_ref, wstate_ref, wnew_ref, o_ref, *win_out):
    w = KV_WIDTH
    q_raw = qraw_ref[i]
    q_rot = qrot_ref[i]
    n_head = q_raw.shape[0]
    head = lax.broadcasted_iota(jnp.int32, (n_head, n_head), 0)
    head2 = lax.broadcasted_iota(jnp.int32, (n_head, n_head), 1)
    same_group = (head // GROUP == head2 // GROUP).astype(bf16)

    s = _mm_nt(q_raw, kc_ref[i])
    blk = lax.broadcasted_iota(jnp.int32, s.shape, 1)
    mask = blk < n_cmp
    s = jnp.where(mask, s, NEG_INF)
    p = jnp.where(mask, jnp.exp(s - jnp.max(s, axis=1, keepdims=True)), 0.0)
    p = p / jnp.sum(p, axis=1, keepdims=True)
    o_c = jnp.dot(p.astype(bf16), vc_ref[i], preferred_element_type=f32)
    imp = _mm_exact_rhs(_mm_exact_lhs(same_group, p), c2s_ref[...])
    cur = jnp.full((n_head, 1), n_sel - 1, jnp.int32)
    sel = _topk_mask(imp, cur, n_sel)

    def attend(q, parts, mask_past, k_new, v_new, new_ok):
        sp = jnp.concatenate([_mm_nt(q, ref[j, :, 0:w].astype(bf16)) for ref, j in parts], axis=1)
        sp = jnp.where(mask_past, sp, NEG_INF)
        sn = jnp.sum(q.astype(f32) * k_new.astype(bf16).astype(f32), axis=1, keepdims=True)
        sn = jnp.where(new_ok, sn, NEG_INF)
        m = jnp.maximum(jnp.max(sp, axis=1, keepdims=True), sn)
        pp = jnp.where(mask_past, jnp.exp(sp - m), 0.0)
        pn = jnp.where(new_ok, jnp.exp(sn - m), 0.0)
        l = jnp.sum(pp, axis=1, keepdims=True) + pn
        pp = pp.astype(bf16)
        acc = pn.astype(bf16).astype(f32) * v_new.astype(bf16).astype(f32)
        rows = 0
        for ref, j in parts:
            n = ref.shape[1]
            acc = acc + jnp.dot(pp[:, rows:rows + n], ref[j, :, w:2 * w].astype(bf16), preferred_element_type=f32)
            rows += n
        return acc / l

    n_past = past_ref.shape[1]
    jj = lax.broadcasted_iota(jnp.int32, (LANES, n_past), 0)
    kk = lax.broadcasted_iota(jnp.int32, (LANES, n_past), 1)
    expand = (jj == kk // L_SEL).astype(bf16)
    picked = jnp.dot(sel.astype(bf16), expand, preferred_element_type=f32) > 0.5
    new_row = new_ref[i]
    o_s = attend(q_rot, [(past_ref, i)], picked, new_row[:, 2 * w:3 * w], new_row[:, 3 * w:4 * w],
                 sel[:, n_sel - 1:n_sel] > 0.5)

    idx = lax.broadcasted_iota(jnp.int32, (n_head, win_buf), 1)
    in_window = win_buf - idx < WINDOW
    wnew = wnew_ref[i]
    o_w = attend(q_rot, [(wstate_ref, i)], in_window, wnew[:, 0:w], wnew[:, w:2 * w], jnp.full((n_head, 1), True))

    gate = gate_ref[i]
    o = gate[:, 0:1] * o_c + gate[:, 1:2] * o_s + gate[:, 2:3] * o_w
    hrow = lax.broadcasted_iota(jnp.int32, o.shape, 0)
    lane = lax.broadcasted_iota(jnp.int32, o.shape, 1)
    o_ref[i] = jnp.where(lane // HEAD_DIM == hrow // GROUP, o, 0.0).astype(o_ref.dtype)

    if emit_win:
        state = wstate_ref[i]
        row = lax.broadcasted_iota(jnp.int32, state.shape, 0)
        win_out[0][i] = jnp.where(row == win_buf - 1, wnew, pltpu.roll(state, win_buf - 1, 0))


def _attn_sample(q_raw, q_rot, gates, kc, vc, c2s, past, new_rows, wstate, wnew, n_cmp, n_sel, emit_win):
    bsz = q_raw.shape[0]
    n_past = past.shape[1]
    win_buf = wstate.shape[1]
    n_seq = SEQ_PER_STEP if bsz % SEQ_PER_STEP == 0 else 1
    per_seq = lambda a: pl.BlockSpec((n_seq,) + a.shape[1:], lambda b: (b,) + (0,) * (a.ndim - 1))
    out_specs = [pl.BlockSpec((n_seq, N_HEADS, KV_WIDTH), lambda b: (b, 0, 0))]
    out_shape = [jax.ShapeDtypeStruct((bsz, N_HEADS, KV_WIDTH), bf16)]
    if emit_win:
        out_specs.append(per_seq(wstate))
        out_shape.append(jax.ShapeDtypeStruct(wstate.shape, wstate.dtype))
    return pl.pallas_call(
        functools.partial(_attn_sample_body, n_cmp, n_sel, win_buf, emit_win),
        grid=(bsz // n_seq,),
        in_specs=[per_seq(q_raw), per_seq(q_rot), per_seq(gates), per_seq(kc), per_seq(vc),
                  pl.BlockSpec(c2s.shape, lambda b: (0, 0)),
                  pl.BlockSpec((n_seq, n_past, 2 * KV_WIDTH), lambda b: (b, 0, 1)), per_seq(new_rows), per_seq(wstate),
                  per_seq(wnew)],
        out_specs=out_specs,
        out_shape=out_shape,
        compiler_params=_cparams(("arbitrary",)),
        name="attn_sample",
    )(q_raw, q_rot, gates, kc, vc, c2s, past, new_rows, wstate, wnew)


def _rope_tables(pos):
    half = ROT_DIM // 2
    inv = np.power(ROPE_THETA, -np.arange(half, dtype=np.float64) * (2.0 / ROT_DIM))
    ang = np.asarray(pos, np.float64)[:, None] * inv[None, :]
    cos_h = np.ones((len(pos), HEAD_DIM), np.float32)
    sin_h = np.zeros((len(pos), HEAD_DIM), np.float32)
    cos_h[:, :half] = np.cos(ang)
    cos_h[:, half:ROT_DIM] = np.cos(ang)
    sin_h[:, :half] = -np.sin(ang)
    sin_h[:, half:ROT_DIM] = np.sin(ang)
    return (jnp.asarray(np.tile(cos_h, (1, 2))), jnp.asarray(np.tile(sin_h, (1, 2))),
            jnp.asarray(cos_h.T.copy()), jnp.asarray(sin_h.T.copy()))


def _cmp_to_sel(t, n_rows, transposed):
    nc = (t - L_CMP) // D_CMP + 1
    ns = -(-t // L_SEL)
    c_start = np.arange(nc) * D_CMP
    s_start = np.arange(ns) * L_SEL
    m = np.zeros((n_rows, LANES), np.float32)
    m[:nc, :ns] = (c_start[:, None] < s_start[None, :] + L_SEL) & (c_start[:, None] + L_CMP > s_start[None, :])
    return jnp.asarray(m.T.copy() if transposed else m, bf16), nc, ns


def _gate_blockdiag(w_gate):
    wg = w_gate.astype(bf16).reshape(2, LRU_HEADS // 2, 2, LRU_BLOCK, LRU_BLOCK)
    z = jnp.zeros_like(wg[:, :, 0])
    top = jnp.concatenate([wg[:, :, 0], z], axis=-1)
    bot = jnp.concatenate([z, wg[:, :, 1]], axis=-1)
    return jnp.concatenate([top, bot], axis=-2)


def _compress_weights(pe_cmp, w_cmp1, b_cmp1, w_cmp2, per_group):
    pe = pe_cmp.reshape(2, 1, L_CMP * HEAD_DIM)
    w1 = w_cmp1.astype(bf16)
    w1p = w1.reshape(2, 2, D_CMP // 2, 2, HEAD_DIM, CMP_HIDDEN)
    eye2 = jnp.eye(2, dtype=bf16)
    wp = jnp.einsum('shpldk,ab->spladbhk', w1p, eye2).reshape(2, D_CMP // 2, 2 * LANES, 4 * CMP_HIDDEN)
    b1 = b_cmp1.reshape(2, 1, CMP_HIDDEN)
    w2 = w_cmp2.astype(bf16)
    eye = jnp.eye(N_KV_HEADS, dtype=bf16)
    if per_group:
        pad = jnp.zeros((CMP_HIDDEN, CMP_HIDDEN - HEAD_DIM), bf16)
        w2x = jnp.stack([jnp.concatenate([w2[0], pad], axis=1), jnp.concatenate([w2[1].T, pad.T], axis=0)])
    else:
        w2x = jnp.einsum('skd,gq->sgkqd', w2, eye).reshape(2, N_KV_HEADS, CMP_HIDDEN, KV_WIDTH)
    return pe, w1, wp, b1, w2x


def _q_weights_prompt(w_qg):
    wqt = w_qg[:, :Q_WIDTH].T.astype(bf16)
    wg = w_qg[:, Q_WIDTH:].T.reshape(N_KV_HEADS, 3 * GROUP, D_MODEL)
    wgt = jnp.pad(wg, ((0, 0), (0, 2 * SUBLANES - 3 * GROUP), (0, 0))).reshape(N_KV_HEADS * 2 * SUBLANES, D_MODEL).astype(bf16)
    return wqt, wgt


def _q_weights_sample(w_qg):
    wq = w_qg[:, :Q_WIDTH].astype(bf16).reshape(D_MODEL, N_KV_HEADS, GROUP, HEAD_DIM)
    eye = jnp.eye(N_KV_HEADS, dtype=bf16)
    wq = jnp.einsum('kgrd,gq->kgrqd', wq, eye).reshape(D_MODEL, N_HEADS * KV_WIDTH)
    wg = w_qg[:, Q_WIDTH:].reshape(D_MODEL, N_HEADS, 3)
    wg = jnp.pad(wg, ((0, 0), (0, 0), (0, LANES - 3))).reshape(D_MODEL, N_HEADS * LANES).astype(bf16)
    return wq, wg


def _out_weights_sample(w_o):
    w = w_o.astype(bf16).reshape(N_KV_HEADS, GROUP, HEAD_DIM, D_MODEL)
    eye = jnp.eye(N_KV_HEADS, dtype=bf16)
    return jnp.einsum('grdn,gq->grqdn', w, eye).reshape(N_HEADS * KV_WIDTH, D_MODEL)


def kernel(x_prompt, x_sample, c_prompt, c_sample, cache_nsa_kv, page_table, state_nsa_win, state_lru_h, state_lru_conv, state_ffn_conv, w_mod, b_mod, g_norm, w_lru_in, b_lru_in, w_lru_conv, b_lru_conv, w_lru_gate, b_lru_gate, lru_lambda, w_lru_out, g_kv, w_mod_kv, b_mod_kv, w_kv, pe_cmp, w_cmp1, b_cmp1, w_cmp2, w_nsa_qg, w_nsa_out, w_ffn_in, w_ffn_conv, b_ffn_conv, w_ffn_out):
    bp, tp, d = x_prompt.shape
    bs, ts, _ = x_sample.shape
    assert ts == 1 and d == D_MODEL
    depth = w_mod.shape[0]
    n_a = w_lru_in.shape[0]
    past_len = page_table.shape[1] * PAGE_SIZE
    win_buf = state_nsa_win.shape[1]

    c_all = jnp.concatenate([c_prompt, c_sample], axis=0)
    mod_all = _ada_mod(c_all, w_mod, b_mod).reshape(depth, bp + bs, N_MOD, d)
    mod_p = mod_all[:, :bp]
    mod_s = jnp.swapaxes(mod_all[:, bp:], 1, 2)
    modkv_all = _ada_mod(c_all, w_mod_kv[None], b_mod_kv[None]).reshape(bp + bs, 2, d)
    modkv_p = modkv_all[:bp]
    modkv_s = jnp.swapaxes(modkv_all[bp:], 0, 1)

    cos_p, sin_p, cos_pt, sin_pt = _rope_tables(np.arange(tp))
    cos_s, sin_s, _, _ = _rope_tables(np.asarray([past_len]))

    x_p = x_prompt
    x_s = x_sample.reshape(bs, d)
    lru_h_p, lru_h_s, lru_c_p, lru_c_s, ffn_c_p, ffn_c_s = [], [], [], [], [], []
    row1 = lambda a: a.reshape(1, -1)

    ctx_p = ctx_s = win_next_s = None
    keep_s = min(WINDOW, past_len + ts)
    for l in range(depth):
        g = g_norm[l]
        wi = w_ffn_in[l].astype(bf16)
        wo = w_ffn_out[l].astype(bf16)
        ffn_taps_s = jnp.swapaxes(state_ffn_conv[l], 0, 1)
        pre_p = pre_s = None
        if l < n_a:
            w_in = w_lru_in[l].astype(bf16)
            w_out = w_lru_out[l].astype(bf16)
            wg = _gate_blockdiag(w_lru_gate[l])
            args = (w_in, row1(b_lru_in[l]), w_lru_conv[l], row1(b_lru_conv[l]), wg, b_lru_gate[l], row1(lru_lambda[l]), w_out)
            x_p, h_last, cbuf = _lru_prompt(x_p, mod_p[l], g, *args)
            lru_h_p.append(h_last[:, 0])
            lru_c_p.append(cbuf[:, SUBLANES - (LRU_CONV_W - 1):])
            taps = jnp.swapaxes(state_lru_conv[l], 0, 1)
            x_s, h_new, up_new = _lru_sample(x_s, mod_s[l], g, state_lru_h[l], taps, *args)
            lru_h_s.append(h_new)
            lru_c_s.append(jnp.concatenate([state_lru_conv[l][:, 1:], up_new[:, None]], axis=1))
        else:
            j = l - n_a
            wq_p, wgt_p = _q_weights_prompt(w_nsa_qg[j])
            q_raw, q_rot, gates = _q_prompt(x_p, mod_p[l], g, wq_p, wgt_p, cos_pt, sin_pt)
            o_att = _attn_prompt(q_raw, q_rot, gates, *ctx_p)
            pre_p = (o_att, w_nsa_out[j].astype(bf16))
            wq_s, wgt_s = _q_weights_sample(w_nsa_qg[j])
            q_raw_s, q_rot_s, gates_s = _q_sample(x_s, mod_s[l], g, wq_s, wgt_s, cos_s, sin_s)
            emit_win = j == 0 and win_buf == keep_s
            res = _attn_sample(q_raw_s.reshape(bs, N_HEADS, KV_WIDTH), q_rot_s.reshape(bs, N_HEADS, KV_WIDTH),
                               gates_s.reshape(bs, N_HEADS, LANES), *ctx_s, emit_win)
            if emit_win:
                win_next_s = res[1]
            pre_s = (res[0].reshape(bs, N_HEADS * KV_WIDTH), _out_weights_sample(w_nsa_out[j]))
        x_p, fbuf = _ffn_prompt(x_p, mod_p[l], g, wi, w_ffn_conv[l], row1(b_ffn_conv[l]), wo, pre=pre_p)
        ffn_c_p.append(fbuf[:, SUBLANES - (FFN_CONV_W - 1):])
        x_s, z_new = _ffn_sample(x_s, mod_s[l], g, ffn_taps_s, wi, w_ffn_conv[l], row1(b_ffn_conv[l]), wo, pre=pre_s)
        ffn_c_s.append(jnp.concatenate([state_ffn_conv[l][:, 1:], z_new[:, None]], axis=1))

        if l == n_a - 1:
            w_kv_b = w_kv.astype(bf16)
            w_vt = jnp.concatenate([w_kv[:, 3 * KV_WIDTH:4 * KV_WIDTH].T, w_kv[:, 5 * KV_WIDTH:6 * KV_WIDTH].T], axis=0).astype(bf16)
            rows_p, win_p, kaug, kwin, vst, vwt = _kv_prompt(x_p, modkv_p, row1(g_kv), w_kv_b, w_vt, cos_p, sin_p)
            kc_p, vct_p = _compress(rows_p, _compress_weights(pe_cmp, w_cmp1, b_cmp1, w_cmp2, True), True)
            c2st_p, nc_p, ns_p = _cmp_to_sel(tp, kc_p.shape[2], True)
            ctx_p = (kc_p, vct_p, c2st_p, kaug, vst, kwin, vwt, nc_p, ns_p)
            rows_s, win_s = _kv_sample(x_s, modkv_s, row1(g_kv), w_kv_b, cos_s, sin_s)
            pages = cache_nsa_kv.reshape(cache_nsa_kv.shape[0], PAGE_SIZE, 4 * KV_WIDTH)
            past = pages[page_table].reshape(bs, past_len, 4 * KV_WIDTH)
            kc_s, vc_s = _compress(past, _compress_weights(pe_cmp, w_cmp1, b_cmp1, w_cmp2, False), False)
            c2s_s, nc_s, ns_s = _cmp_to_sel(past_len + ts, kc_s.shape[1], False)
            ctx_s = (kc_s, vc_s, c2s_s, past, rows_s.reshape(bs, 1, 4 * KV_WIDTH),
                     state_nsa_win.reshape(bs, win_buf, 2 * KV_WIDTH), win_s.reshape(bs, 1, 2 * KV_WIDTH), nc_s, ns_s)

    keep_p = min(WINDOW, tp)
    kv_rows_p = rows_p.reshape(bp, tp, 4, N_KV_HEADS, HEAD_DIM)
    win_out_p = win_p[:, tp - keep_p:].reshape(bp, keep_p, 2, N_KV_HEADS, HEAD_DIM)
    kv_rows_s = rows_s.reshape(bs, ts, 4, N_KV_HEADS, HEAD_DIM)
    if win_next_s is not None:
        win_out_s = win_next_s.reshape(bs, keep_s, 2, N_KV_HEADS, HEAD_DIM)
    else:
        win_all = jnp.concatenate([state_nsa_win, win_s.reshape(bs, ts, 2, N_KV_HEADS, HEAD_DIM)], axis=1)
        win_out_s = win_all[:, win_buf + ts - keep_s:]
    return (x_p, x_s.reshape(bs, ts, d), kv_rows_p, kv_rows_s, win_out_p, win_out_s,
            jnp.stack(lru_h_p), jnp.stack(lru_h_s), jnp.stack(lru_c_p), jnp.stack(lru_c_s),
            jnp.stack(ffn_c_p), jnp.stack(ffn_c_s))
```

```python
import functools

import numpy as np
import jax
import jax.numpy as jnp
from jax import lax
from jax.experimental import pallas as pl
from jax.experimental.pallas import tpu as pltpu

f32 = jnp.float32
bf16 = jnp.bfloat16

D_MODEL = 1024
D_RNN = D_MODEL
LRU_HEADS = 8
LRU_BLOCK = D_RNN // LRU_HEADS
LRU_CONV_W = 4
LRU_C = 8.0
N_HEADS = 16
HEAD_DIM = 64
N_KV_HEADS = 4
GROUP = N_HEADS // N_KV_HEADS
Q_WIDTH = N_HEADS * HEAD_DIM
KV_WIDTH = N_KV_HEADS * HEAD_DIM
L_CMP = 32
D_CMP = 16
CMP_HIDDEN = 128
L_SEL = 64
TOP_N = 8
WINDOW = 512
ROT_DIM = HEAD_DIM // 4
ROPE_THETA = 500000.0
D_FF = 3 * D_MODEL
FFN_CONV_W = 3
N_MOD = 6
EPS = 1e-6
NEG_INF = -1e30
FORCED_SCORE = 1e6
PAGE_SIZE = 128

LANES = 128
SUBLANES = 8
VMEM_LIMIT = 60 * 1024 * 1024
ROW_TILE = 256
FFN_TILE = 512
Q_TILE = 512
K_CHUNK = 512
FF_CHUNK = 3072
BELOW_ALL = -3.0e38
SEQ_PER_STEP = 4
SEL_GROUP = 16


def _cparams(sem=None):
    return pltpu.CompilerParams(dimension_semantics=sem, vmem_limit_bytes=VMEM_LIMIT)


def _const_spec(shape):
    nd = len(shape)
    return pl.BlockSpec(shape, lambda *_: (0,) * nd, pipeline_mode=pl.Buffered(1))


def _rms(x, g):
    return x * lax.rsqrt(jnp.mean(x * x, axis=-1, keepdims=True) + EPS) * g


def _gelu(x):
    return 0.5 * x * (1.0 + jnp.tanh(0.7978845608028654 * (x + 0.044715 * (x * x * x))))


def _mm(a, w):
    return jnp.dot(a.astype(bf16), w, preferred_element_type=f32)


def _mm_nt(a, b):
    return lax.dot_general(a, b, (((1,), (1,)), ((), ())), preferred_element_type=f32)


def _mm_exact_rhs(a, b01):
    a1 = a.astype(bf16)
    r1 = a - a1.astype(f32)
    a2 = r1.astype(bf16)
    a3 = (r1 - a2.astype(f32)).astype(bf16)
    return (jnp.dot(a1, b01, preferred_element_type=f32) + jnp.dot(a2, b01, preferred_element_type=f32)
            + jnp.dot(a3, b01, preferred_element_type=f32))


def _mm_exact_lhs(a01, b):
    b1 = b.astype(bf16)
    r1 = b - b1.astype(f32)
    b2 = r1.astype(bf16)
    b3 = (r1 - b2.astype(f32)).astype(bf16)
    return (jnp.dot(a01, b1, preferred_element_type=f32) + jnp.dot(a01, b2, preferred_element_type=f32)
            + jnp.dot(a01, b3, preferred_element_type=f32))


def _shift_rows(cur, tail, k):
    r = pltpu.roll(cur, k, 0)
    rp = pltpu.roll(tail, k, 0)
    row = lax.broadcasted_iota(jnp.int32, tail.shape, 0)
    first = jnp.where(row < k, rp, r[0:SUBLANES])
    return jnp.concatenate([first, r[SUBLANES:]], axis=0)


def _causal_conv_tile(cur, tail, w, b):
    width = w.shape[0]
    out = b + w[width - 1:width] * cur
    for k in range(1, width):
        out = out + w[width - 1 - k:width - k] * _shift_rows(cur, tail, k)
    return out


def _lru_gates(u, wg_ref, bg, lam):
    pair = 2 * LRU_BLOCK
    n_pair = D_RNN // pair
    ub = u.astype(bf16)
    gates = []
    for gi in range(2):
        cols = [jnp.dot(ub[:, p * pair:(p + 1) * pair], wg_ref[gi, p], preferred_element_type=f32) for p in range(n_pair)]
        gates.append(jnp.concatenate(cols, axis=1) + bg[gi:gi + 1])
    r = jax.nn.sigmoid(gates[0])
    i = jax.nn.sigmoid(gates[1])
    z = -lam
    softplus = jnp.maximum(z, 0.0) + jnp.log1p(jnp.exp(-jnp.abs(z)))
    log_a = -LRU_C * r * softplus
    a = jnp.exp(log_a)
    mult = jnp.sqrt(-jnp.tanh(log_a) * (a * a + 1.0))
    return a, mult * (i * u)


def _scan_rows(a, b, h0):
    n, c = a.shape
    n_grp = n // SUBLANES
    a = a.reshape(n_grp, SUBLANES, c)
    b = b.reshape(n_grp, SUBLANES, c)
    row = lax.broadcasted_iota(jnp.int32, a.shape, 1)
    s = 1
    while s < SUBLANES:
        a_sh = jnp.where(row >= s, pltpu.roll(a, s, 1), 1.0)
        b_sh = jnp.where(row >= s, pltpu.roll(b, s, 1), 0.0)
        b = a * b_sh + b
        a = a * a_sh
        s *= 2
    out = []
    carry = h0
    for i in range(n_grp):
        h = b[i] + a[i] * carry
        out.append(h)
        carry = h[SUBLANES - 1:SUBLANES]
    return jnp.concatenate(out, axis=0)


def _rope_cols(x, cos_t, sin_t):
    lane = lax.broadcasted_iota(jnp.int32, x.shape, 1)
    half = ROT_DIM // 2
    swapped = jnp.where(lane % HEAD_DIM < half, pltpu.roll(x, LANES - half, 1), pltpu.roll(x, half, 1))
    return x * cos_t + swapped * sin_t


def _rope_wide(x, cos_t, sin_t):
    n = x.shape[1] // LANES
    return jnp.concatenate([_rope_cols(x[:, j * LANES:(j + 1) * LANES], cos_t, sin_t) for j in range(n)], axis=1)


def _topk_mask(imp, cur, n_blocks):
    blk = lax.broadcasted_iota(jnp.int32, imp.shape, 1)
    forced = (blk == 0) | (blk == cur) | (blk == cur - 1)
    imp = jnp.where(forced, FORCED_SCORE, imp)
    imp = jnp.where(blk <= cur, imp, NEG_INF)
    v = jnp.where(blk < n_blocks, imp, BELOW_ALL)
    rank = jnp.zeros(imp.shape, jnp.int32)
    for i in range(n_blocks):
        vi = v[:, i:i + 1]
        rank = rank + ((vi > v) | ((vi == v) & (blk > i))).astype(jnp.int32)
    return jnp.where((rank < TOP_N) & (v > 0.5 * NEG_INF), 1.0, 0.0)


def _mod_body(c_ref, w_ref, b_ref, o_ref):
    c = c_ref[...]
    o_ref[0] = _mm(c * jax.nn.sigmoid(c), w_ref[0].astype(bf16)) + b_ref[0]


def _ada_mod(c, w, b):
    n_l, d, n = w.shape
    m = c.shape[0]
    tn = n // 2 if n % (2 * LANES) == 0 else n
    return pl.pallas_call(
        _mod_body,
        grid=(n_l, n // tn),
        in_specs=[pl.BlockSpec((m, d), lambda l, j: (0, 0)),
                  pl.BlockSpec((1, d, tn), lambda l, j: (l, 0, j)),
                  pl.BlockSpec((1, 1, tn), lambda l, j: (l, 0, j))],
        out_specs=pl.BlockSpec((1, m, tn), lambda l, j: (l, 0, j)),
        out_shape=jax.ShapeDtypeStruct((n_l, m, n), f32),
        compiler_params=_cparams(("arbitrary", "arbitrary")),
        name="ada_mod",
    )(c, w, b.reshape(n_l, 1, n))


def _lru_prompt_body(x_ref, mod_ref, g_ref, win_ref, bin_ref, wc_ref, bc_ref, wg_ref, bg_ref, lam_ref, wout_ref,
                     xo_ref, hlast_ref, cbuf_ref, tail_ref, h_ref):
    t = pl.program_id(1)

    @pl.when(t == 0)
    def _():
        tail_ref[...] = jnp.zeros(tail_ref.shape, f32)
        h_ref[...] = jnp.zeros(h_ref.shape, f32)

    x = x_ref[0]
    tm = x.shape[0]
    mod = mod_ref[0]
    g = g_ref[...]
    h = _rms(x, g[0:1]) * (1.0 + mod[1:2]) + mod[0:1]
    proj = _mm(h, win_ref[...]) + bin_ref[...]
    y = _gelu(proj[:, :D_RNN])
    up = proj[:, D_RNN:]
    u = _causal_conv_tile(up, tail_ref[...], wc_ref[...], bc_ref[...])
    last_rows = up[tm - SUBLANES:tm]
    tail_ref[...] = last_rows
    cbuf_ref[0] = last_rows
    a, bx = _lru_gates(u, wg_ref, bg_ref[...], lam_ref[...])
    hs = _scan_rows(a, bx, h_ref[0:1])
    h_fin = hs[tm - 1:tm]
    h_ref[...] = jnp.broadcast_to(h_fin, h_ref.shape)
    hlast_ref[0] = jnp.broadcast_to(h_fin, hlast_ref.shape[1:])
    out = _mm(hs * y, wout_ref[...])
    xo_ref[0] = x + mod[2:3] * _rms(out, g[1:2])


def _lru_prompt(x, mod, g, w_in, b_in, w_conv, b_conv, wg, bg, lam, w_out):
    bsz, t, d = x.shape
    tm = min(ROW_TILE, t)
    row_spec = pl.BlockSpec((1, tm, d), lambda b, i: (b, i, 0))
    small = lambda r, c: pl.BlockSpec((1, r, c), lambda b, i: (b, 0, 0))
    return pl.pallas_call(
        _lru_prompt_body,
        grid=(bsz, t // tm),
        in_specs=[row_spec, small(N_MOD, d), _const_spec(g.shape), _const_spec(w_in.shape), _const_spec(b_in.shape),
                  _const_spec(w_conv.shape), _const_spec(b_conv.shape), _const_spec(wg.shape), _const_spec(bg.shape),
                  _const_spec(lam.shape), _const_spec(w_out.shape)],
        out_specs=[row_spec, small(SUBLANES, D_RNN), small(SUBLANES, D_RNN)],
        out_shape=[jax.ShapeDtypeStruct((bsz, t, d), f32), jax.ShapeDtypeStruct((bsz, SUBLANES, D_RNN), f32),
                   jax.ShapeDtypeStruct((bsz, SUBLANES, D_RNN), f32)],
        scratch_shapes=[pltpu.VMEM((SUBLANES, D_RNN), f32), pltpu.VMEM((SUBLANES, D_RNN), f32)],
        compiler_params=_cparams(("arbitrary", "arbitrary")),
        name="lru_prompt",
    )(x, mod, g, w_in, b_in, w_conv, b_conv, wg, bg, lam, w_out)


def _lru_sample_body(x_ref, mod_ref, g_ref, h0_ref, taps_ref, win_ref, bin_ref, wc_ref, bc_ref, wg_ref, bg_ref, lam_ref,
                     wout_ref, xo_ref, hnew_ref, up_ref):
    x = x_ref[...]
    g = g_ref[...]
    h = _rms(x, g[0:1]) * (1.0 + mod_ref[1]) + mod_ref[0]
    proj = _mm(h, win_ref[...]) + bin_ref[...]
    y = _gelu(proj[:, :D_RNN])
    up = proj[:, D_RNN:]
    up_ref[...] = up
    wc = wc_ref[...]
    u = bc_ref[...] + wc[LRU_CONV_W - 1:LRU_CONV_W] * up
    for k in range(LRU_CONV_W - 1):
        u = u + wc[k:k + 1] * taps_ref[k]
    a, bx = _lru_gates(u, wg_ref, bg_ref[...], lam_ref[...])
    hs = a * h0_ref[...] + bx
    hnew_ref[...] = hs
    out = _mm(hs * y, wout_ref[...])
    xo_ref[...] = x + mod_ref[2] * _rms(out, g[1:2])


def _lru_sample(x, mod, g, h0, taps, w_in, b_in, w_conv, b_conv, wg, bg, lam, w_out):
    m, d = x.shape
    return pl.pallas_call(
        _lru_sample_body,
        out_shape=[jax.ShapeDtypeStruct((m, d), f32), jax.ShapeDtypeStruct((m, D_RNN), f32),
                   jax.ShapeDtypeStruct((m, D_RNN), f32)],
        compiler_params=_cparams(),
        name="lru_sample",
    )(x, mod, g, h0, taps, w_in, b_in, w_conv, b_conv, wg, bg, lam, w_out)


def _ffn_core(h, tap_fn, wi_ref, wc, bc, wo_ref, store_fn):
    acc = jnp.zeros((h.shape[0], D_MODEL), f32)
    hb = h.astype(bf16)
    for c in range(D_FF // FF_CHUNK):
        zs = []
        for base in (0, D_FF):
            lo = base + c * FF_CHUNK
            z = jnp.dot(hb, wi_ref[:, lo:lo + FF_CHUNK], preferred_element_type=f32)
            store_fn(z, lo)
            zc = bc[:, lo:lo + FF_CHUNK] + wc[FFN_CONV_W - 1:FFN_CONV_W, lo:lo + FF_CHUNK] * z
            for k in range(1, FFN_CONV_W):
                zc = zc + wc[FFN_CONV_W - 1 - k:FFN_CONV_W - k, lo:lo + FF_CHUNK] * tap_fn(z, lo, k)
            zs.append(zc)
        gz = _gelu(zs[0]) * zs[1]
        acc = acc + jnp.dot(gz.astype(bf16), wo_ref[c * FF_CHUNK:(c + 1) * FF_CHUNK, :], preferred_element_type=f32)
    return acc


def _ffn_prompt_body(has_pre, *refs):
    if has_pre:
        (x_ref, o_ref, wpre_ref, mod_ref, g_ref, wi_ref, wc_ref, bc_ref, wo_ref, xo_ref, fbuf_ref, tail_ref) = refs
    else:
        (x_ref, mod_ref, g_ref, wi_ref, wc_ref, bc_ref, wo_ref, xo_ref, fbuf_ref, tail_ref) = refs
    t = pl.program_id(1)

    @pl.when(t == 0)
    def _():
        tail_ref[...] = jnp.zeros(tail_ref.shape, f32)

    x = x_ref[0]
    tm = x.shape[0]
    mod = mod_ref[0]
    g = g_ref[...]
    if has_pre:
        x = x + mod[2:3] * _rms(jnp.dot(o_ref[0], wpre_ref[...], preferred_element_type=f32), g[1:2])
    h = _rms(x, g[2:3]) * (1.0 + mod[4:5]) + mod[3:4]

    def tap(z, lo, k):
        return _shift_rows(z, tail_ref[:, lo:lo + FF_CHUNK], k)

    def store(z, lo):
        fbuf_ref[0, :, lo:lo + FF_CHUNK] = z[tm - SUBLANES:tm]

    acc = _ffn_core(h, tap, wi_ref, wc_ref[...], bc_ref[...], wo_ref, store)
    tail_ref[...] = fbuf_ref[0]
    xo_ref[0] = x + mod[5:6] * _rms(acc, g[3:4])


def _ffn_prompt(x, mod, g, w_in, w_conv, b_conv, w_out, pre=None):
    bsz, t, d = x.shape
    tm = min(FFN_TILE, t)
    row_spec = pl.BlockSpec((1, tm, d), lambda b, i: (b, i, 0))
    small = lambda r, c: pl.BlockSpec((1, r, c), lambda b, i: (b, 0, 0))
    ins, specs = [x], [row_spec]
    if pre is not None:
        o_att, w_pre = pre
        ins += [o_att, w_pre]
        specs += [pl.BlockSpec((1, tm, o_att.shape[2]), lambda b, i: (b, i, 0)), _const_spec(w_pre.shape)]
    ins += [mod, g, w_in, w_conv, b_conv, w_out]
    specs += [small(N_MOD, d), _const_spec(g.shape), _const_spec(w_in.shape), _const_spec(w_conv.shape),
              _const_spec(b_conv.shape), _const_spec(w_out.shape)]
    return pl.pallas_call(
        functools.partial(_ffn_prompt_body, pre is not None),
        grid=(bsz, t // tm),
        in_specs=specs,
        out_specs=[row_spec, small(SUBLANES, 2 * D_FF)],
        out_shape=[jax.ShapeDtypeStruct((bsz, t, d), f32), jax.ShapeDtypeStruct((bsz, SUBLANES, 2 * D_FF), f32)],
        scratch_shapes=[pltpu.VMEM((SUBLANES, 2 * D_FF), f32)],
        compiler_params=_cparams(("arbitrary", "arbitrary")),
        name="ffn_prompt",
    )(*ins)


def _ffn_sample_body(has_pre, *refs):
    if has_pre:
        (x_ref, o_ref, wpre_ref, mod_ref, g_ref, taps_ref, wi_ref, wc_ref, bc_ref, wo_ref, xo_ref, z_ref) = refs
    else:
        (x_ref, mod_ref, g_ref, taps_ref, wi_ref, wc_ref, bc_ref, wo_ref, xo_ref, z_ref) = refs
    x = x_ref[...]
    g = g_ref[...]
    if has_pre:
        x = x + mod_ref[2] * _rms(jnp.dot(o_ref[...], wpre_ref[...], preferred_element_type=f32), g[1:2])
    h = _rms(x, g[2:3]) * (1.0 + mod_ref[4]) + mod_ref[3]

    def tap(z, lo, k):
        return taps_ref[FFN_CONV_W - 1 - k, :, lo:lo + FF_CHUNK]

    def store(z, lo):
        z_ref[:, lo:lo + FF_CHUNK] = z

    acc = _ffn_core(h, tap, wi_ref, wc_ref[...], bc_ref[...], wo_ref, store)
    xo_ref[...] = x + mod_ref[5] * _rms(acc, g[3:4])


def _ffn_sample(x, mod, g, taps, w_in, w_conv, b_conv, w_out, pre=None):
    m, d = x.shape
    ins = [x] + (list(pre) if pre is not None else []) + [mod, g, taps, w_in, w_conv, b_conv, w_out]
    return pl.pallas_call(
        functools.partial(_ffn_sample_body, pre is not None),
        out_shape=[jax.ShapeDtypeStruct((m, d), f32), jax.ShapeDtypeStruct((m, 2 * D_FF), f32)],
        compiler_params=_cparams(),
        name="ffn_sample",
    )(*ins)


def _head_cols(src, extra):
    lane = lax.broadcasted_iota(jnp.int32, (src.shape[0], LANES), 1)
    out = []
    for j in range(KV_WIDTH // LANES):
        col = src[:, j * LANES:(j + 1) * LANES]
        out.append(jnp.where(lane < HEAD_DIM, col, extra))
        out.append(jnp.where(lane < HEAD_DIM, pltpu.roll(col, HEAD_DIM, 1), extra))
    return out


def _kv_prompt_body(x_ref, mod_ref, g_ref, w_ref, wvt_ref, cos_ref, sin_ref, rows_ref, win_ref, kaug_ref, kwin_ref, vst_ref, vwt_ref):
    x = x_ref[0]
    tm = x.shape[0]
    mod = mod_ref[0]
    h = _rms(x, g_ref[...]) * (1.0 + mod[1:2]) + mod[0:1]
    hb = h.astype(bf16)
    kv = jnp.dot(hb, w_ref[...], preferred_element_type=f32)
    vt = _mm_nt(wvt_ref[...], hb)
    cos_t, sin_t = cos_ref[...], sin_ref[...]
    w = KV_WIDTH
    k_sel = _rope_wide(kv[:, 2 * w:3 * w], cos_t, sin_t)
    k_win = _rope_wide(kv[:, 4 * w:5 * w], cos_t, sin_t)
    rows_ref[0, :, 0:2 * w] = kv[:, 0:2 * w]
    rows_ref[0, :, 2 * w:3 * w] = k_sel
    rows_ref[0, :, 3 * w:4 * w] = kv[:, 3 * w:4 * w]
    win_ref[0, :, 0:w] = k_win
    win_ref[0, :, w:2 * w] = kv[:, 5 * w:6 * w]
    row = lax.broadcasted_iota(jnp.int32, (tm, LANES), 0)
    lane = lax.broadcasted_iota(jnp.int32, (tm, LANES), 1)
    pos = pl.program_id(1) * tm + row
    onehot = jnp.where(lane - HEAD_DIM == (pos // L_SEL) % SEL_GROUP, 1.0, 0.0)
    for gi, col in enumerate(_head_cols(k_sel, onehot)):
        for c in range(tm // K_CHUNK):
            kaug_ref[0, gi, c] = col[c * K_CHUNK:(c + 1) * K_CHUNK].astype(bf16)
    for gi, col in enumerate(_head_cols(k_win, 0.0)):
        for c in range(tm // Q_TILE):
            kwin_ref[0, gi, c] = col[c * Q_TILE:(c + 1) * Q_TILE, 0:HEAD_DIM].astype(bf16)
    for gi in range(N_KV_HEADS):
        v_s = vt[gi * HEAD_DIM:(gi + 1) * HEAD_DIM].astype(bf16)
        v_w = vt[w + gi * HEAD_DIM:w + (gi + 1) * HEAD_DIM].astype(bf16)
        for c in range(tm // K_CHUNK):
            vst_ref[0, gi, c] = v_s[:, c * K_CHUNK:(c + 1) * K_CHUNK]
        for c in range(tm // Q_TILE):
            vwt_ref[0, gi, c] = v_w[:, c * Q_TILE:(c + 1) * Q_TILE]


def _kv_prompt(x, mod, g_kv, w_kv, w_vt, cos_t, sin_t):
    bsz, t, d = x.shape
    tm = K_CHUNK
    nc, nt = tm // K_CHUNK, tm // Q_TILE
    row = lambda c: pl.BlockSpec((1, tm, c), lambda b, i: (b, i, 0))
    chunked = lambda n, r, c: pl.BlockSpec((1, N_KV_HEADS, n, r, c), lambda b, i: (b, 0, i, 0, 0))
    shape5 = lambda n, r, c: jax.ShapeDtypeStruct((bsz, N_KV_HEADS, n, r, c), bf16)
    return pl.pallas_call(
        _kv_prompt_body,
        grid=(bsz, t // tm),
        in_specs=[row(d), pl.BlockSpec((1, 2, d), lambda b, i: (b, 0, 0)), _const_spec(g_kv.shape), _const_spec(w_kv.shape),
                  _const_spec(w_vt.shape), pl.BlockSpec((tm, LANES), lambda b, i: (i, 0)),
                  pl.BlockSpec((tm, LANES), lambda b, i: (i, 0))],
        out_specs=[row(4 * KV_WIDTH), row(2 * KV_WIDTH), chunked(nc, K_CHUNK, LANES), chunked(nt, Q_TILE, HEAD_DIM),
                   chunked(nc, HEAD_DIM, K_CHUNK), chunked(nt, HEAD_DIM, Q_TILE)],
        out_shape=[jax.ShapeDtypeStruct((bsz, t, 4 * KV_WIDTH), f32), jax.ShapeDtypeStruct((bsz, t, 2 * KV_WIDTH), f32),
                   shape5(t // K_CHUNK, K_CHUNK, LANES), shape5(t // Q_TILE, Q_TILE, HEAD_DIM),
                   shape5(t // K_CHUNK, HEAD_DIM, K_CHUNK), shape5(t // Q_TILE, HEAD_DIM, Q_TILE)],
        compiler_params=_cparams(("arbitrary", "arbitrary")),
        name="kv_prompt",
    )(x, mod, g_kv, w_kv, w_vt, cos_t, sin_t)


def _kv_sample_body(x_ref, mod_ref, g_ref, w_ref, cos_ref, sin_ref, rows_ref, win_ref):
    h = _rms(x_ref[...], g_ref[...]) * (1.0 + mod_ref[1]) + mod_ref[0]
    kv = _mm(h, w_ref[...])
    cos_t, sin_t = cos_ref[...], sin_ref[...]
    w = KV_WIDTH
    rows_ref[:, 0:2 * w] = kv[:, 0:2 * w]
    rows_ref[:, 2 * w:3 * w] = _rope_wide(kv[:, 2 * w:3 * w], cos_t, sin_t)
    rows_ref[:, 3 * w:4 * w] = kv[:, 3 * w:4 * w]
    win_ref[:, 0:w] = _rope_wide(kv[:, 4 * w:5 * w], cos_t, sin_t)
    win_ref[:, w:2 * w] = kv[:, 5 * w:6 * w]


def _kv_sample(x, mod, g_kv, w_kv, cos_t, sin_t):
    m = x.shape[0]
    return pl.pallas_call(
        _kv_sample_body,
        out_shape=[jax.ShapeDtypeStruct((m, 4 * KV_WIDTH), f32), jax.ShapeDtypeStruct((m, 2 * KV_WIDTH), f32)],
        compiler_params=_cparams(),
        name="kv_sample",
    )(x, mod, g_kv, w_kv, cos_t, sin_t)


def _compress_body(per_group, *refs):
    n_slab = 2 * KV_WIDTH // LANES
    slabs = refs[:n_slab]
    pe_ref, w1_ref, wp_ref, b1_ref, w2_ref = refs[n_slab:n_slab + 5]
    outs = refs[n_slab + 5:]
    n_chunk = slabs[0].shape[1] // D_CMP
    rows_of = lambda j, start: slabs[j][0, pl.ds(start, n_chunk, stride=D_CMP), :]
    per_slot = KV_WIDTH // LANES
    hid = CMP_HIDDEN
    for s in range(2):
        const = _mm(jnp.broadcast_to(pe_ref[s], (SUBLANES, L_CMP * HEAD_DIM)), w1_ref[s])[0:1] + b1_ref[s]
        tok = jnp.zeros((n_chunk, KV_WIDTH), f32)
        for j in range(per_slot):
            acc = jnp.zeros((n_chunk, 4 * hid), f32)
            for rp in range(D_CMP // 2):
                lhs = jnp.concatenate([rows_of(s * per_slot + j, 2 * rp), rows_of(s * per_slot + j, 2 * rp + 1)], axis=1)
                acc = acc + _mm(lhs, wp_ref[s, rp])
            for hl in range(2):
                gi = 2 * j + hl
                first = acc[:, 2 * hl * hid:(2 * hl + 1) * hid]
                second = acc[:, (2 * hl + 1) * hid:(2 * hl + 2) * hid]
                act_g = _gelu(first + pltpu.roll(second, n_chunk - 1, 0) + const).astype(bf16)
                if not per_group:
                    tok = tok + jnp.dot(act_g, w2_ref[s, gi], preferred_element_type=f32)
                elif s == 0:
                    outs[s][0, gi] = jnp.dot(act_g, w2_ref[s], preferred_element_type=f32)[:, 0:HEAD_DIM].astype(bf16)
                else:
                    outs[s][0, gi] = _mm_nt(w2_ref[s], act_g)[0:HEAD_DIM].astype(bf16)
        if not per_group:
            outs[s][0] = tok.astype(bf16)


def _compress(rows, cw, per_group):
    bsz, t = rows.shape[:2]
    n_chunk = t // D_CMP
    n_slab = 2 * KV_WIDTH // LANES
    slab_specs = [pl.BlockSpec((1, t, LANES), functools.partial(lambda j, b: (b, 0, j), j)) for j in range(n_slab)]
    pe, wt, wb, b1, w2 = cw
    if per_group:
        dims = [(n_chunk, HEAD_DIM), (HEAD_DIM, n_chunk)]
        out_specs = [pl.BlockSpec((1, N_KV_HEADS) + dm, lambda b: (b, 0, 0, 0)) for dm in dims]
        out_shapes = [jax.ShapeDtypeStruct((bsz, N_KV_HEADS) + dm, bf16) for dm in dims]
    else:
        out_specs = [pl.BlockSpec((1, n_chunk, KV_WIDTH), lambda b: (b, 0, 0))] * 2
        out_shapes = [jax.ShapeDtypeStruct((bsz, n_chunk, KV_WIDTH), bf16)] * 2
    return pl.pallas_call(
        functools.partial(_compress_body, per_group),
        grid=(bsz,),
        in_specs=slab_specs + [_const_spec(pe.shape), _const_spec(wt.shape), _const_spec(wb.shape), _const_spec(b1.shape),
                               _const_spec(w2.shape)],
        out_specs=out_specs,
        out_shape=out_shapes,
        compiler_params=_cparams(("arbitrary",)),
        name="compress_prompt" if per_group else "compress_sample",
    )(*([rows] * n_slab), pe, wt, wb, b1, w2)


def _q_prompt_body(x_ref, mod_ref, g_ref, wqt_ref, wgt_ref, cos_ref, sin_ref, qraw_ref, qrot_ref, gate_ref):
    mod = mod_ref[0]
    h = _rms(x_ref[0], g_ref[0:1]) * (1.0 + mod[1:2]) + mod[0:1]
    hb = h.astype(bf16)
    qt = _mm_nt(wqt_ref[...], hb)
    gate_ref[0] = jax.nn.sigmoid(_mm_nt(wgt_ref[...], hb))
    scale = HEAD_DIM ** -0.5
    qraw_ref[0] = (qt * scale).astype(bf16)
    half = ROT_DIM // 2
    dim = lax.broadcasted_iota(jnp.int32, qt.shape, 0) % HEAD_DIM
    swapped = jnp.where(dim < half, pltpu.roll(qt, Q_WIDTH - half, 0), pltpu.roll(qt, half, 0))
    cos_t = jnp.concatenate([cos_ref[...]] * N_HEADS, axis=0)
    sin_t = jnp.concatenate([sin_ref[...]] * N_HEADS, axis=0)
    qrot_ref[0] = ((qt * cos_t + swapped * sin_t) * scale).astype(bf16)


def _q_prompt(x, mod, g, wqt, wgt, cos_tt, sin_tt):
    bsz, t, d = x.shape
    tm = min(ROW_TILE, t)
    col = lambda r: pl.BlockSpec((1, r, tm), lambda b, i: (b, 0, i))
    tab = pl.BlockSpec((HEAD_DIM, tm), lambda b, i: (0, i))
    return pl.pallas_call(
        _q_prompt_body,
        grid=(bsz, t // tm),
        in_specs=[pl.BlockSpec((1, tm, d), lambda b, i: (b, i, 0)), pl.BlockSpec((1, N_MOD, d), lambda b, i: (b, 0, 0)),
                  _const_spec(g.shape), _const_spec(wqt.shape), _const_spec(wgt.shape), tab, tab],
        out_specs=[col(Q_WIDTH), col(Q_WIDTH), col(wgt.shape[0])],
        out_shape=[jax.ShapeDtypeStruct((bsz, Q_WIDTH, t), bf16), jax.ShapeDtypeStruct((bsz, Q_WIDTH, t), bf16),
                   jax.ShapeDtypeStruct((bsz, wgt.shape[0], t), f32)],
        compiler_params=_cparams(("arbitrary", "arbitrary")),
        name="q_prompt",
    )(x, mod, g, wqt, wgt, cos_tt, sin_tt)


def _q_sample_body(x_ref, mod_ref, g_ref, wq_ref, wgt_ref, cos_ref, sin_ref, qraw_ref, qrot_ref, gate_ref):
    h = _rms(x_ref[...], g_ref[0:1]) * (1.0 + mod_ref[1]) + mod_ref[0]
    hb = h.astype(bf16)
    q = jnp.dot(hb, wq_ref[...], preferred_element_type=f32)
    gate_ref[...] = jax.nn.sigmoid(jnp.dot(hb, wgt_ref[...], preferred_element_type=f32))
    scale = HEAD_DIM ** -0.5
    qraw_ref[...] = (q * scale).astype(bf16)
    qrot_ref[...] = (_rope_wide(q, cos_ref[...], sin_ref[...]) * scale).astype(bf16)


def _q_sample(x, mod, g, wq, wgt, cos_t, sin_t):
    m = x.shape[0]
    return pl.pallas_call(
        _q_sample_body,
        out_shape=[jax.ShapeDtypeStruct((m, wq.shape[1]), bf16), jax.ShapeDtypeStruct((m, wq.shape[1]), bf16),
                   jax.ShapeDtypeStruct((m, wgt.shape[1]), f32)],
        compiler_params=_cparams(),
        name="q_sample",
    )(x, mod, g, wq, wgt, cos_t, sin_t)


def _rank_select(imp, cur, n_blocks):
    blk = lax.broadcasted_iota(jnp.int32, imp.shape, 0)
    forced = (blk == 0) | (blk == cur) | (blk == cur - 1)
    v = jnp.where(forced, FORCED_SCORE, imp)
    v = jnp.where(blk <= cur, v, NEG_INF)
    v = jnp.where(blk < n_blocks, v, BELOW_ALL)
    rank = jnp.zeros(imp.shape, jnp.int32)
    for i in range(n_blocks):
        vi = v[i:i + 1, :]
        rank = rank + ((vi > v) | ((vi == v) & (blk > i))).astype(jnp.int32)
    return (rank < TOP_N) & (v > 0.5 * NEG_INF)


def _attn_prompt_body(n_cmp, n_sel, qraw_ref, qrot_ref, gate_ref, kc_ref, vct_ref, c2st_ref, kaug_ref, vst_ref, kwin_ref,
                      vwt_ref, o_ref, qaug_ref, negsel_ref):
    qi = pl.program_id(2)
    tq = qraw_ref.shape[2]
    hq = tq // 2
    n4 = GROUP * tq
    nh = GROUP * hq
    q_chunk = qi

    def stack(ref):
        return jnp.concatenate([ref[0, r * HEAD_DIM:(r + 1) * HEAD_DIM, h * hq:(h + 1) * hq]
                                for h in range(2) for r in range(GROUP)], axis=1)

    def per_query(x):
        return jnp.concatenate([x[:, h * hq:(h + 1) * hq] for h in range(2) for _ in range(GROUP)], axis=1)

    q_raw = stack(qraw_ref)
    q_rot = stack(qrot_ref)
    lane = lax.broadcasted_iota(jnp.int32, (1, n4), 1)
    q_pos = qi * tq + (lane // nh) * hq + lane % hq

    s = jnp.dot(kc_ref[0, 0], q_raw, preferred_element_type=f32)
    blk = lax.broadcasted_iota(jnp.int32, s.shape, 0)
    mask = (blk * D_CMP + (L_CMP - 1) <= q_pos) & (blk < n_cmp)
    s = jnp.where(mask, s, NEG_INF)
    p = jnp.where(mask, jnp.exp(s - jnp.max(s, axis=0, keepdims=True)), 0.0)
    l = jnp.sum(p, axis=0, keepdims=True)
    p = p / jnp.where(l > 0.0, l, 1.0)
    o_c = jnp.dot(vct_ref[0, 0], p.astype(bf16), preferred_element_type=f32)

    halves = []
    for h in range(2):
        acc_h = p[:, h * nh:h * nh + hq]
        for r in range(1, GROUP):
            acc_h = acc_h + p[:, h * nh + r * hq:h * nh + (r + 1) * hq]
        halves.append(acc_h)
    p_group = jnp.concatenate(halves, axis=1)
    imp = _mm_exact_lhs(c2st_ref[...], p_group)
    n_rows = negsel_ref.shape[0]
    cur = (qi * tq + lax.broadcasted_iota(jnp.int32, (1, tq), 1)) // L_SEL
    picked = _rank_select(imp[0:n_rows], cur, n_sel)
    negsel_ref[...] = per_query(jnp.where(picked, 0.0, NEG_INF))

    qaug_ref[0:HEAD_DIM, :] = q_rot
    qaug_ref[HEAD_DIM + SEL_GROUP:, :] = jnp.zeros((LANES - HEAD_DIM - SEL_GROUP, n4), bf16)
    blocks_per_chunk = K_CHUNK // L_SEL

    def load_sel_rows(c):
        first = pl.multiple_of((c * blocks_per_chunk // SEL_GROUP) * SEL_GROUP, SEL_GROUP)
        qaug_ref[HEAD_DIM:HEAD_DIM + SEL_GROUP, :] = negsel_ref[pl.ds(first, SEL_GROUP), :].astype(bf16)

    def chunk_scores(c):
        load_sel_rows(c)
        return jnp.dot(kaug_ref[0, 0, c], qaug_ref[...], preferred_element_type=f32)

    def update(sc, carry, vt):
        m_old, l_old, acc_old = carry
        m_new = jnp.maximum(m_old, jnp.max(sc, axis=0, keepdims=True))
        alpha = jnp.exp(m_old - m_new)
        pc = jnp.exp(sc - m_new)
        l_new = alpha * l_old + jnp.sum(pc, axis=0, keepdims=True)
        acc_new = alpha * acc_old + jnp.dot(vt, pc.astype(bf16), preferred_element_type=f32)
        return m_new, l_new, acc_new

    init = (jnp.full((1, n4), NEG_INF, f32), jnp.zeros((1, n4), f32), jnp.zeros((HEAD_DIM, n4), f32))
    carry = lax.fori_loop(0, q_chunk, lambda c, cr: update(chunk_scores(c), cr, vst_ref[0, 0, c]), init)
    load_sel_rows(q_chunk)
    k_own = kaug_ref[0, 0, q_chunk]
    v_own = vst_ref[0, 0, q_chunk]
    l_parts, acc_parts = [], []
    for h in range(2):
        n_keys = (h + 1) * hq
        lanes = slice(h * nh, (h + 1) * nh)
        sc = jnp.dot(k_own[0:n_keys], qaug_ref[:, lanes], preferred_element_type=f32)
        key_pos = q_chunk * K_CHUNK + lax.broadcasted_iota(jnp.int32, sc.shape, 0)
        sc = jnp.where(key_pos <= q_pos[:, lanes], sc, NEG_INF)
        _, l_h, acc_h = update(sc, tuple(x[:, lanes] for x in carry), v_own[:, 0:n_keys])
        l_parts.append(l_h)
        acc_parts.append(acc_h)
    l_s = jnp.concatenate(l_parts, axis=1)
    acc_s = jnp.concatenate(acc_parts, axis=1)

    prev = jnp.maximum(qi - 1, 0)
    no_prev = jnp.where(qi >= 1, 0.0, NEG_INF)
    k_prev, k_cur = kwin_ref[0, 0, prev], kwin_ref[0, 0, qi]
    v_prev, v_cur = vwt_ref[0, 0, prev], vwt_ref[0, 0, qi]
    l_parts, acc_parts = [], []
    for h in range(2):
        lanes = slice(h * nh, (h + 1) * nh)
        q_h = q_rot[:, lanes]
        rows_prev = slice(h * hq, tq)
        rows_cur = slice(0, (h + 1) * hq)
        s_prev = jnp.dot(k_prev[rows_prev], q_h, preferred_element_type=f32) + no_prev
        s_cur = jnp.dot(k_cur[rows_cur], q_h, preferred_element_type=f32)
        col = h * hq + lax.broadcasted_iota(jnp.int32, (1, nh), 1) % hq
        row_prev = h * hq + lax.broadcasted_iota(jnp.int32, s_prev.shape, 0)
        row_cur = lax.broadcasted_iota(jnp.int32, s_cur.shape, 0)
        sw = jnp.concatenate([jnp.where(row_prev > col, s_prev, NEG_INF), jnp.where(row_cur <= col, s_cur, NEG_INF)], axis=0)
        pw = jnp.exp(sw - jnp.max(sw, axis=0, keepdims=True))
        l_parts.append(jnp.sum(pw, axis=0, keepdims=True))
        v_h = jnp.concatenate([v_prev[:, rows_prev], v_cur[:, rows_cur]], axis=1)
        acc_parts.append(jnp.dot(v_h, pw.astype(bf16), preferred_element_type=f32))
    l_w = jnp.concatenate(l_parts, axis=1)
    acc_w = jnp.concatenate(acc_parts, axis=1)

    gate = gate_ref[0]
    o_s = acc_s / l_s
    o_w = acc_w / l_w
    heads = []
    for r in range(GROUP):
        parts = []
        for h in range(2):
            sl = slice(h * nh + r * hq, h * nh + (r + 1) * hq)
            gt = gate[:, h * hq:(h + 1) * hq]
            parts.append(gt[3 * r:3 * r + 1] * o_c[:, sl] + gt[3 * r + 1:3 * r + 2] * o_s[:, sl]
                         + gt[3 * r + 2:3 * r + 3] * o_w[:, sl])
        heads.append(jnp.concatenate(parts, axis=1))
    o_ref[0] = jnp.concatenate(heads, axis=0).T.astype(o_ref.dtype)


def _attn_prompt(qt_raw, qt_rot, gates_t, kc, vct, c2st, kaug, vst, kwin, vwt, n_cmp, n_sel):
    bsz, _, t = qt_raw.shape
    tq = Q_TILE
    assert tq == K_CHUNK == WINDOW and t % tq == 0
    gw = GROUP * HEAD_DIM
    n4 = GROUP * tq
    q_spec = pl.BlockSpec((1, gw, tq), lambda b, g, i: (b, g, i))
    gate_rows = gates_t.shape[1] // N_KV_HEADS
    per_group = lambda a: pl.BlockSpec((1, 1) + a.shape[2:], lambda b, g, i: (b, g) + (0,) * (a.ndim - 2))
    sel_rows = -(-n_sel // SEL_GROUP) * SEL_GROUP
    return pl.pallas_call(
        functools.partial(_attn_prompt_body, n_cmp, n_sel),
        grid=(bsz, N_KV_HEADS, t // tq),
        in_specs=[q_spec, q_spec, pl.BlockSpec((1, gate_rows, tq), lambda b, g, i: (b, g, i)), per_group(kc), per_group(vct),
                  pl.BlockSpec(c2st.shape, lambda b, g, i: (0, 0)), per_group(kaug), per_group(vst), per_group(kwin),
                  per_group(vwt)],
        out_specs=pl.BlockSpec((1, tq, gw), lambda b, g, i: (b, i, g)),
        out_shape=jax.ShapeDtypeStruct((bsz, t, Q_WIDTH), bf16),
        scratch_shapes=[pltpu.VMEM((LANES, n4), bf16), pltpu.VMEM((sel_rows, n4), f32)],
        compiler_params=_cparams(("arbitrary", "arbitrary", "arbitrary")),
        name="attn_prompt",
    )(qt_raw, qt_rot, gates_t, kc, vct, c2st, kaug, vst, kwin, vwt)


def _attn_sample_body(n_cmp, n_sel, win_buf, emit_win, *refs):
    for i in range(refs[0].shape[0]):
        _attn_sample_one(n_cmp, n_sel, win_buf, emit_win, i, *refs)


def _attn_sample_one(n_cmp, n_sel, win_buf, emit_win, i, qraw_ref, qrot_ref, gate_ref, kc_ref, vc_ref, c2s_ref, past_ref,
                     new_ref, wstate_ref, wnew_ref, o_ref, *win_out):
    w = KV_WIDTH
    q_raw = qraw_ref[i]
    q_rot = qrot_ref[i]
    n_head = q_raw.shape[0]
    head = lax.broadcasted_iota(jnp.int32, (n_head, n_head), 0)
    head2 = lax.broadcasted_iota(jnp.int32, (n_head, n_head), 1)
    same_group = (head // GROUP == head2 // GROUP).astype(bf16)

    s = _mm_nt(q_raw, kc_ref[i])
    blk = lax.broadcasted_iota(jnp.int32, s.shape, 1)
    mask = blk < n_cmp
    s = jnp.where(mask, s, NEG_INF)
    p = jnp.where(mask, jnp.exp(s - jnp.max(s, axis=1, keepdims=True)), 0.0)
    p = p / jnp.sum(p, axis=1, keepdims=True)
    o_c = jnp.dot(p.astype(bf16), vc_ref[i], preferred_element_type=f32)
    imp = _mm_exact_rhs(_mm_exact_lhs(same_group, p), c2s_ref[...])
    cur = jnp.full((n_head, 1), n_sel - 1, jnp.int32)
    sel = _topk_mask(imp, cur, n_sel)

    def attend(q, parts, mask_past, k_new, v_new, new_ok):
        sp = jnp.concatenate([_mm_nt(q, ref[j, :, 0:w].astype(bf16)) for ref, j in parts], axis=1)
        sp = jnp.where(mask_past, sp, NEG_INF)
        sn = jnp.sum(q.astype(f32) * k_new.astype(bf16).astype(f32), axis=1, keepdims=True)
        sn = jnp.where(new_ok, sn, NEG_INF)
        m = jnp.maximum(jnp.max(sp, axis=1, keepdims=True), sn)
        pp = jnp.where(mask_past, jnp.exp(sp - m), 0.0)
        pn = jnp.where(new_ok, jnp.exp(sn - m), 0.0)
        l = jnp.sum(pp, axis=1, keepdims=True) + pn
        pp = pp.astype(bf16)
        acc = pn.astype(bf16).astype(f32) * v_new.astype(bf16).astype(f32)
        rows = 0
        for ref, j in parts:
            n = ref.shape[1]
            acc = acc + jnp.dot(pp[:, rows:rows + n], ref[j, :, w:2 * w].astype(bf16), preferred_element_type=f32)
            rows += n
        return acc / l

    n_past = past_ref.shape[1]
    jj = lax.broadcasted_iota(jnp.int32, (LANES, n_past), 0)
    kk = lax.broadcasted_iota(jnp.int32, (LANES, n_past), 1)
    expand = (jj == kk // L_SEL).astype(bf16)
    picked = jnp.dot(sel.astype(bf16), expand, preferred_element_type=f32) > 0.5
    new_row = new_ref[i]
    o_s = attend(q_rot, [(past_ref, i)], picked, new_row[:, 2 * w:3 * w], new_row[:, 3 * w:4 * w],
                 sel[:, n_sel - 1:n_sel] > 0.5)

    idx = lax.broadcasted_iota(jnp.int32, (n_head, win_buf), 1)
    in_window = win_buf - idx < WINDOW
    wnew = wnew_ref[i]
    o_w = attend(q_rot, [(wstate_ref, i)], in_window, wnew[:, 0:w], wnew[:, w:2 * w], jnp.full((n_head, 1), True))

    gate = gate_ref[i]
    o = gate[:, 0:1] * o_c + gate[:, 1:2] * o_s + gate[:, 2:3] * o_w
    hrow = lax.broadcasted_iota(jnp.int32, o.shape, 0)
    lane = lax.broadcasted_iota(jnp.int32, o.shape, 1)
    o_ref[i] = jnp.where(lane // HEAD_DIM == hrow // GROUP, o, 0.0).astype(o_ref.dtype)

    if emit_win:
        state = wstate_ref[i]
        row = lax.broadcasted_iota(jnp.int32, state.shape, 0)
        win_out[0][i] = jnp.where(row == win_buf - 1, wnew, pltpu.roll(state, win_buf - 1, 0))


def _attn_sample(q_raw, q_rot, gates, kc, vc, c2s, past, new_rows, wstate, wnew, n_cmp, n_sel, emit_win):
    bsz = q_raw.shape[0]
    n_past = past.shape[1]
    win_buf = wstate.shape[1]
    n_seq = SEQ_PER_STEP if bsz % SEQ_PER_STEP == 0 else 1
    per_seq = lambda a: pl.BlockSpec((n_seq,) + a.shape[1:], lambda b: (b,) + (0,) * (a.ndim - 1))
    out_specs = [pl.BlockSpec((n_seq, N_HEADS, KV_WIDTH), lambda b: (b, 0, 0))]
    out_shape = [jax.ShapeDtypeStruct((bsz, N_HEADS, KV_WIDTH), bf16)]
    if emit_win:
        out_specs.append(per_seq(wstate))
        out_shape.append(jax.ShapeDtypeStruct(wstate.shape, wstate.dtype))
    return pl.pallas_call(
        functools.partial(_attn_sample_body, n_cmp, n_sel, win_buf, emit_win),
        grid=(bsz // n_seq,),
        in_specs=[per_seq(q_raw), per_seq(q_rot), per_seq(gates), per_seq(kc), per_seq(vc),
                  pl.BlockSpec(c2s.shape, lambda b: (0, 0)),
                  pl.BlockSpec((n_seq, n_past, 2 * KV_WIDTH), lambda b: (b, 0, 1)), per_seq(new_rows), per_seq(wstate),
                  per_seq(wnew)],
        out_specs=out_specs,
        out_shape=out_shape,
        compiler_params=_cparams(("arbitrary",)),
        name="attn_sample",
    )(q_raw, q_rot, gates, kc, vc, c2s, past, new_rows, wstate, wnew)


def _rope_tables(pos):
    half = ROT_DIM // 2
    inv = np.power(ROPE_THETA, -np.arange(half, dtype=np.float64) * (2.0 / ROT_DIM))
    ang = np.asarray(pos, np.float64)[:, None] * inv[None, :]
    cos_h = np.ones((len(pos), HEAD_DIM), np.float32)
    sin_h = np.zeros((len(pos), HEAD_DIM), np.float32)
    cos_h[:, :half] = np.cos(ang)
    cos_h[:, half:ROT_DIM] = np.cos(ang)
    sin_h[:, :half] = -np.sin(ang)
    sin_h[:, half:ROT_DIM] = np.sin(ang)
    return (jnp.asarray(np.tile(cos_h, (1, 2))), jnp.asarray(np.tile(sin_h, (1, 2))),
            jnp.asarray(cos_h.T.copy()), jnp.asarray(sin_h.T.copy()))


def _cmp_to_sel(t, n_rows, transposed):
    nc = (t - L_CMP) // D_CMP + 1
    ns = -(-t // L_SEL)
    c_start = np.arange(nc) * D_CMP
    s_start = np.arange(ns) * L_SEL
    m = np.zeros((n_rows, LANES), np.float32)
    m[:nc, :ns] = (c_start[:, None] < s_start[None, :] + L_SEL) & (c_start[:, None] + L_CMP > s_start[None, :])
    return jnp.asarray(m.T.copy() if transposed else m, bf16), nc, ns


def _gate_blockdiag(w_gate):
    wg = w_gate.astype(bf16).reshape(2, LRU_HEADS // 2, 2, LRU_BLOCK, LRU_BLOCK)
    z = jnp.zeros_like(wg[:, :, 0])
    top = jnp.concatenate([wg[:, :, 0], z], axis=-1)
    bot = jnp.concatenate([z, wg[:, :, 1]], axis=-1)
    return jnp.concatenate([top, bot], axis=-2)


def _compress_weights(pe_cmp, w_cmp1, b_cmp1, w_cmp2, per_group):
    pe = pe_cmp.reshape(2, 1, L_CMP * HEAD_DIM)
    w1 = w_cmp1.astype(bf16)
    w1p = w1.reshape(2, 2, D_CMP // 2, 2, HEAD_DIM, CMP_HIDDEN)
    eye2 = jnp.eye(2, dtype=bf16)
    wp = jnp.einsum('shpldk,ab->spladbhk', w1p, eye2).reshape(2, D_CMP // 2, 2 * LANES, 4 * CMP_HIDDEN)
    b1 = b_cmp1.reshape(2, 1, CMP_HIDDEN)
    w2 = w_cmp2.astype(bf16)
    eye = jnp.eye(N_KV_HEADS, dtype=bf16)
    if per_group:
        pad = jnp.zeros((CMP_HIDDEN, CMP_HIDDEN - HEAD_DIM), bf16)
        w2x = jnp.stack([jnp.concatenate([w2[0], pad], axis=1), jnp.concatenate([w2[1].T, pad.T], axis=0)])
    else:
        w2x = jnp.einsum('skd,gq->sgkqd', w2, eye).reshape(2, N_KV_HEADS, CMP_HIDDEN, KV_WIDTH)
    return pe, w1, wp, b1, w2x


def _q_weights_prompt(w_qg):
    wqt = w_qg[:, :Q_WIDTH].T.astype(bf16)
    wg = w_qg[:, Q_WIDTH:].T.reshape(N_KV_HEADS, 3 * GROUP, D_MODEL)
    wgt = jnp.pad(wg, ((0, 0), (0, 2 * SUBLANES - 3 * GROUP), (0, 0))).reshape(N_KV_HEADS * 2 * SUBLANES, D_MODEL).astype(bf16)
    return wqt, wgt


def _q_weights_sample(w_qg):
    wq = w_qg[:, :Q_WIDTH].astype(bf16).reshape(D_MODEL, N_KV_HEADS, GROUP, HEAD_DIM)
    eye = jnp.eye(N_KV_HEADS, dtype=bf16)
    wq = jnp.einsum('kgrd,gq->kgrqd', wq, eye).reshape(D_MODEL, N_HEADS * KV_WIDTH)
    wg = w_qg[:, Q_WIDTH:].reshape(D_MODEL, N_HEADS, 3)
    wg = jnp.pad(wg, ((0, 0), (0, 0), (0, LANES - 3))).reshape(D_MODEL, N_HEADS * LANES).astype(bf16)
    return wq, wg


def _out_weights_sample(w_o):
    w = w_o.astype(bf16).reshape(N_KV_HEADS, GROUP, HEAD_DIM, D_MODEL)
    eye = jnp.eye(N_KV_HEADS, dtype=bf16)
    return jnp.einsum('grdn,gq->grqdn', w, eye).reshape(N_HEADS * KV_WIDTH, D_MODEL)


def kernel(x_prompt, x_sample, c_prompt, c_sample, cache_nsa_kv, page_table, state_nsa_win, state_lru_h, state_lru_conv, state_ffn_conv, w_mod, b_mod, g_norm, w_lru_in, b_lru_in, w_lru_conv, b_lru_conv, w_lru_gate, b_lru_gate, lru_lambda, w_lru_out, g_kv, w_mod_kv, b_mod_kv, w_kv, pe_cmp, w_cmp1, b_cmp1, w_cmp2, w_nsa_qg, w_nsa_out, w_ffn_in, w_ffn_conv, b_ffn_conv, w_ffn_out):
    bp, tp, d = x_prompt.shape
    bs, ts, _ = x_sample.shape
    assert ts == 1 and d == D_MODEL
    depth = w_mod.shape[0]
    n_a = w_lru_in.shape[0]
    past_len = page_table.shape[1] * PAGE_SIZE
    win_buf = state_nsa_win.shape[1]

    c_all = jnp.concatenate([c_prompt, c_sample], axis=0)
    mod_all = _ada_mod(c_all, w_mod, b_mod).reshape(depth, bp + bs, N_MOD, d)
    mod_p = mod_all[:, :bp]
    mod_s = jnp.swapaxes(mod_all[:, bp:], 1, 2)
    modkv_all = _ada_mod(c_all, w_mod_kv[None], b_mod_kv[None]).reshape(bp + bs, 2, d)
    modkv_p = modkv_all[:bp]
    modkv_s = jnp.swapaxes(modkv_all[bp:], 0, 1)

    cos_p, sin_p, cos_pt, sin_pt = _rope_tables(np.arange(tp))
    cos_s, sin_s, _, _ = _rope_tables(np.asarray([past_len]))

    x_p = x_prompt
    x_s = x_sample.reshape(bs, d)
    lru_h_p, lru_h_s, lru_c_p, lru_c_s, ffn_c_p, ffn_c_s = [], [], [], [], [], []
    row1 = lambda a: a.reshape(1, -1)

    ctx_p = ctx_s = win_next_s = None
    keep_s = min(WINDOW, past_len + ts)
    for l in range(depth):
        g = g_norm[l]
        wi = w_ffn_in[l].astype(bf16)
        wo = w_ffn_out[l].astype(bf16)
        ffn_taps_s = jnp.swapaxes(state_ffn_conv[l], 0, 1)
        pre_p = pre_s = None
        if l < n_a:
            w_in = w_lru_in[l].astype(bf16)
            w_out = w_lru_out[l].astype(bf16)
            wg = _gate_blockdiag(w_lru_gate[l])
            args = (w_in, row1(b_lru_in[l]), w_lru_conv[l], row1(b_lru_conv[l]), wg, b_lru_gate[l], row1(lru_lambda[l]), w_out)
            x_p, h_last, cbuf = _lru_prompt(x_p, mod_p[l], g, *args)
            lru_h_p.append(h_last[:, 0])
            lru_c_p.append(cbuf[:, SUBLANES - (LRU_CONV_W - 1):])
            taps = jnp.swapaxes(state_lru_conv[l], 0, 1)
            x_s, h_new, up_new = _lru_sample(x_s, mod_s[l], g, state_lru_h[l], taps, *args)
            lru_h_s.append(h_new)
            lru_c_s.append(jnp.concatenate([state_lru_conv[l][:, 1:], up_new[:, None]], axis=1))
        else:
            j = l - n_a
            wq_p, wgt_p = _q_weights_prompt(w_nsa_qg[j])
            q_raw, q_rot, gates = _q_prompt(x_p, mod_p[l], g, wq_p, wgt_p, cos_pt, sin_pt)
            o_att = _attn_prompt(q_raw, q_rot, gates, *ctx_p)
            pre_p = (o_att, w_nsa_out[j].astype(bf16))
            wq_s, wgt_s = _q_weights_sample(w_nsa_qg[j])
            q_raw_s, q_rot_s, gates_s = _q_sample(x_s, mod_s[l], g, wq_s, wgt_s, cos_s, sin_s)
            emit_win = j == 0 and win_buf == keep_s
            res = _attn_sample(q_raw_s.reshape(bs, N_HEADS, KV_WIDTH), q_rot_s.reshape(bs, N_HEADS, KV_WIDTH),
                               gates_s.reshape(bs, N_HEADS, LANES), *ctx_s, emit_win)
            if emit_win:
                win_next_s = res[1]
            pre_s = (res[0].reshape(bs, N_HEADS * KV_WIDTH), _out_weights_sample(w_nsa_out[j]))
        x_p, fbuf = _ffn_prompt(x_p, mod_p[l], g, wi, w_ffn_conv[l], row1(b_ffn_conv[l]), wo, pre=pre_p)
        ffn_c_p.append(fbuf[:, SUBLANES - (FFN_CONV_W - 1):])
        x_s, z_new = _ffn_sample(x_s, mod_s[l], g, ffn_taps_s, wi, w_ffn_conv[l], row1(b_ffn_conv[l]), wo, pre=pre_s)
        ffn_c_s.append(jnp.concatenate([state_ffn_conv[l][:, 1:], z_new[:, None]], axis=1))

        if l == n_a - 1:
            w_kv_b = w_kv.astype(bf16)
            w_vt = jnp.concatenate([w_kv[:, 3 * KV_WIDTH:4 * KV_WIDTH].T, w_kv[:, 5 * KV_WIDTH:6 * KV_WIDTH].T], axis=0).astype(bf16)
            rows_p, win_p, kaug, kwin, vst, vwt = _kv_prompt(x_p, modkv_p, row1(g_kv), w_kv_b, w_vt, cos_p, sin_p)
            kc_p, vct_p = _compress(rows_p, _compress_weights(pe_cmp, w_cmp1, b_cmp1, w_cmp2, True), True)
            c2st_p, nc_p, ns_p = _cmp_to_sel(tp, kc_p.shape[2], True)
            ctx_p = (kc_p, vct_p, c2st_p, kaug, vst, kwin, vwt, nc_p, ns_p)
            rows_s, win_s = _kv_sample(x_s, modkv_s, row1(g_kv), w_kv_b, cos_s, sin_s)
            pages = cache_nsa_kv.reshape(cache_nsa_kv.shape[0], PAGE_SIZE, 4 * KV_WIDTH)
            past = pages[page_table].reshape(bs, past_len, 4 * KV_WIDTH)
            kc_s, vc_s = _compress(past, _compress_weights(pe_cmp, w_cmp1, b_cmp1, w_cmp2, False), False)
            c2s_s, nc_s, ns_s = _cmp_to_sel(past_len + ts, kc_s.shape[1], False)
            ctx_s = (kc_s, vc_s, c2s_s, past, rows_s.reshape(bs, 1, 4 * KV_WIDTH),
                     state_nsa_win.reshape(bs, win_buf, 2 * KV_WIDTH), win_s.reshape(bs, 1, 2 * KV_WIDTH), nc_s, ns_s)

    keep_p = min(WINDOW, tp)
    kv_rows_p = rows_p.reshape(bp, tp, 4, N_KV_HEADS, HEAD_DIM)
    win_out_p = win_p[:, tp - keep_p:].reshape(bp, keep_p, 2, N_KV_HEADS, HEAD_DIM)
    kv_rows_s = rows_s.reshape(bs, ts, 4, N_KV_HEADS, HEAD_DIM)
    if win_next_s is not None:
        win_out_s = win_next_s.reshape(bs, keep_s, 2, N_KV_HEADS, HEAD_DIM)
    else:
        win_all = jnp.concatenate([state_nsa_win, win_s.reshape(bs, ts, 2, N_KV_HEADS, HEAD_DIM)], axis=1)
        win_out_s = win_all[:, win_buf + ts - keep_s:]
    return (x_p, x_s.reshape(bs, ts, d), kv_rows_p, kv_rows_s, win_out_p, win_out_s,
            jnp.stack(lru_h_p), jnp.stack(lru_h_s), jnp.stack(lru_c_p), jnp.stack(lru_c_s),
            jnp.stack(ffn_c_p), jnp.stack(ffn_c_s))
```
